```python
import math
import jax
import jax.numpy as jnp
from jax import lax
import numpy as np

D_MODEL = 1024
BATCH = 2
SEQ = 16384
DEPTH = 2

GRID_W = 64
CTX_LEN = 256
HEAD_DIM = 64
NA_HEADS = 8
NA_WIN_ROWS = 8
NA_WIN_COLS = 16
MLA_HEADS = 8
MLA_Q_RANK = 256
MLA_KV_RANK = 128
MLA_NOPE = 64
MLA_ROPE = 32
MLA_V = 64
DIFF_HEADS = 4
DIFF_D = 64
DIFF_V = 2 * DIFF_D
A_W = NA_HEADS * HEAD_DIM
B_W = MLA_HEADS * MLA_V
C_QK = DIFF_HEADS * 2 * DIFF_D
C_W = DIFF_HEADS * DIFF_V
IN_SPLITS = (A_W, A_W, A_W, MLA_Q_RANK, MLA_KV_RANK, MLA_ROPE, C_QK, C_QK, C_W)
IN_DIM = sum(IN_SPLITS)
N_BRANCH = 3
N_EXPERTS = 16
EC_CAPACITY_FACTOR = 2
EXPERT_FF = 2816
QUERY_BLOCK = 128
ROPE_BASE = 10000.0
DEEPNORM_ALPHA = (2 * DEPTH) ** 0.25
DEEPNORM_BETA = (8 * DEPTH) ** -0.25
EPS = 1e-6

kernel_name = 'hybrid_natten_mla_diff_ecmoe_dit'


def layer_norm(x, g=None, b=None):
    xf = x.astype(jnp.float32)
    mu = jnp.mean(xf, axis=-1, keepdims=True)
    var = jnp.mean(jnp.square(xf - mu), axis=-1, keepdims=True)
    y = (xf - mu) * lax.rsqrt(var + EPS)
    if g is not None:
        y = y * g.astype(jnp.float32) + b.astype(jnp.float32)
    return y.astype(x.dtype)


def rms_norm(x, g):
    xf = x.astype(jnp.float32)
    y = xf * lax.rsqrt(jnp.mean(jnp.square(xf), axis=-1, keepdims=True) + EPS)
    return (y * g.astype(jnp.float32)).astype(x.dtype)


def modulate(x, shift, scale):
    return layer_norm(x) * (1 + scale) + shift


def to_heads(t, n_heads):
    return t.reshape(t.shape[0], t.shape[1], n_heads, -1)


def split_in(z):
    offs = np.cumsum(IN_SPLITS)[:-1].tolist()
    return jnp.split(z, offs, axis=-1)


def axial_angles(n_tokens, rot_dim):
    t = jnp.arange(n_tokens, dtype=jnp.int32)
    row = (t // GRID_W).astype(jnp.float32)
    col = (t % GRID_W).astype(jnp.float32)
    m = rot_dim // 2
    inv = ROPE_BASE ** (-jnp.arange(0, m, 2, dtype=jnp.float32) / m)
    return row[:, None] * inv, col[:, None] * inv


def rope_1d(x, ang):
    cos = jnp.cos(ang)[None, :, None, :].astype(x.dtype)
    sin = jnp.sin(ang)[None, :, None, :].astype(x.dtype)
    x1, x2 = jnp.split(x, 2, axis=-1)
    return jnp.concatenate([x1 * cos - x2 * sin, x1 * sin + x2 * cos], axis=-1)


def rope_2d(x, ang_row, ang_col):
    m = x.shape[-1] // 2
    return jnp.concatenate([rope_1d(x[..., :m], ang_row), rope_1d(x[..., m:], ang_col)], axis=-1)


def softmax_attend(q, k, v, scale):
    s = jnp.einsum('bqhd,bkhd->bhqk', q, k) * scale
    p = jax.nn.softmax(s.astype(jnp.float32), axis=-1).astype(v.dtype)
    return jnp.einsum('bhqk,bkhd->bqhd', p, v)


def diff_attend(q1, q2, k1, k2, v, lam, scale):
    p1 = jax.nn.softmax((jnp.einsum('bqhd,bkhd->bhqk', q1, k1) * scale).astype(jnp.float32), axis=-1)
    p2 = jax.nn.softmax((jnp.einsum('bqhd,bkhd->bhqk', q2, k2) * scale).astype(jnp.float32), axis=-1)
    a = (p1 - lam * p2).astype(v.dtype)
    return jnp.einsum('bhqk,bkhd->bqhd', a, v)


def sweep_query_blocks(fn, *qs):
    b, n = qs[0].shape[:2]
    nb = n // QUERY_BLOCK
    blocks = tuple(jnp.moveaxis(q.reshape(b, nb, QUERY_BLOCK, *q.shape[2:]), 1, 0) for q in qs)
    out = lax.map(lambda args: fn(*args), blocks)
    return jnp.moveaxis(out, 0, 1).reshape(b, n, *out.shape[3:])


def neighborhood_attend(q, k, v, k_ctx, v_ctx, rpb):
    b, n, h, d = q.shape
    rows = n // GRID_W
    kr = min(NA_WIN_ROWS, rows)
    kc = NA_WIN_COLS
    scale = d ** -0.5
    kg = k.reshape(b, rows, GRID_W, h, d)
    vg = v.reshape(b, rows, GRID_W, h, d)
    r_idx = jnp.arange(rows, dtype=jnp.int32)
    r_start = jnp.clip(r_idx - kr // 2, 0, rows - kr)
    c_idx = jnp.arange(GRID_W, dtype=jnp.int32)
    c_start = jnp.clip(c_idx - kc // 2, 0, GRID_W - kc)
    key_cols = c_start[:, None] + jnp.arange(kc, dtype=jnp.int32)[None, :]
    col_off = key_cols - c_idx[:, None] + (NA_WIN_COLS - 1)
    q_rows = jnp.moveaxis(q.reshape(b, rows, GRID_W, h, d), 1, 0)

    def one_row(args):
        qr, r, rs = args
        kw = lax.dynamic_slice_in_dim(kg, rs, kr, axis=1)[:, :, key_cols]
        vw = lax.dynamic_slice_in_dim(vg, rs, kr, axis=1)[:, :, key_cols]
        row_off = rs + jnp.arange(kr, dtype=jnp.int32) - r + (NA_WIN_ROWS - 1)
        bias = rpb[:, row_off][:, :, col_off]
        s_win = jnp.einsum('bchd,brckhd->bhcrk', qr, kw) * scale + jnp.transpose(bias, (0, 2, 1, 3))[None].astype(qr.dtype)
        s_ctx = jnp.einsum('bchd,blhd->bhcl', qr, k_ctx) * scale
        s = jnp.concatenate([s_win.reshape(b, h, GRID_W, kr * kc), s_ctx], axis=-1)
        p = jax.nn.softmax(s.astype(jnp.float32), axis=-1).astype(v.dtype)
        p_win = p[..., :kr * kc].reshape(b, h, GRID_W, kr, kc)
        p_ctx = p[..., kr * kc:]
        return (jnp.einsum('bhcrk,brckhd->bchd', p_win, vw)
                + jnp.einsum('bhcl,blhd->bchd', p_ctx, v_ctx))

    out = lax.map(one_row, (q_rows, r_idx, r_start))
    return jnp.moveaxis(out, 0, 1).reshape(b, n, h, d)


def mla_heads(cq, ckv, k_rope, g_q, g_kv, w_uq, w_ukv):
    q = to_heads(rms_norm(cq, g_q) @ w_uq, MLA_HEADS)
    kv = to_heads(rms_norm(ckv, g_kv) @ w_ukv, MLA_HEADS)
    return (q[..., :MLA_NOPE], q[..., MLA_NOPE:], kv[..., :MLA_NOPE], k_rope[:, :, None, :],
            kv[..., MLA_NOPE:])


def mla_qk(q_nope, q_rope, k_nope, k_rope):
    q = jnp.concatenate([q_nope, q_rope], axis=-1)
    k = jnp.concatenate([k_nope, jnp.broadcast_to(k_rope, k_nope.shape[:3] + (MLA_ROPE,))], axis=-1)
    return q, k


def diff_qk(t):
    t = t.reshape(t.shape[0], t.shape[1], DIFF_HEADS, 2, DIFF_D)
    return t[..., 0, :], t[..., 1, :]


def gated_merge(h, ya, yb, yc, w_br_a, w_br_b, w_br_c, w_gate, b_gate, w_out):
    ga, gb, gc = jnp.split(jax.nn.sigmoid(h @ w_gate + b_gate), N_BRANCH, axis=-1)
    flat = lambda y: y.reshape(y.shape[0], y.shape[1], -1)
    m = ga * (flat(ya) @ w_br_a) + gb * (flat(yb) @ w_br_b) + gc * (flat(yc) @ w_br_c)
    return m @ w_out


def token_mixer(h, hc, lam_init, w_in, rpb, g_q, g_kv, w_uq, w_ukv, lq1, lk1, lq2, lk2, g_sub,
                w_br_a, w_br_b, w_br_c, w_gate, b_gate, w_out, with_ctx_out):
    n = h.shape[1]
    z = split_in(h @ w_in)
    zc = split_in(hc @ w_in)
    ang_b = axial_angles(n, MLA_ROPE)
    ang_c = axial_angles(n, DIFF_D)

    qa, ka, va = (to_heads(t, NA_HEADS) for t in z[0:3])
    qa_c, ka_c, va_c = (to_heads(t, NA_HEADS) for t in zc[0:3])
    ya = neighborhood_attend(qa, ka, va, ka_c, va_c, rpb)

    qn, qr, kn, krp, vb = mla_heads(z[3], z[4], z[5], g_q, g_kv, w_uq, w_ukv)
    qb, kb = mla_qk(qn, rope_2d(qr, *ang_b), kn, rope_2d(krp, *ang_b))
    qn_c, qr_c, kn_c, krp_c, vb_c = mla_heads(zc[3], zc[4], zc[5], g_q, g_kv, w_uq, w_ukv)
    qb_c, kb_c = mla_qk(qn_c, qr_c, kn_c, krp_c)
    kb_all = jnp.concatenate([kb_c, kb], axis=1)
    vb_all = jnp.concatenate([vb_c, vb], axis=1)
    scale_b = (MLA_NOPE + MLA_ROPE) ** -0.5
    yb = sweep_query_blocks(lambda q: softmax_attend(q, kb_all, vb_all, scale_b), qb)

    lam = (jnp.exp(jnp.sum(lq1.astype(jnp.float32) * lk1.astype(jnp.float32)))
           - jnp.exp(jnp.sum(lq2.astype(jnp.float32) * lk2.astype(jnp.float32))) + lam_init)
    q1, q2 = diff_qk(z[6])
    k1, k2 = diff_qk(z[7])
    q1, q2, k1, k2 = (rope_2d(t, *ang_c) for t in (q1, q2, k1, k2))
    vc = to_heads(z[8], DIFF_HEADS)
    q1_c, q2_c = diff_qk(zc[6])
    k1_c, k2_c = diff_qk(zc[7])
    vc_c = to_heads(zc[8], DIFF_HEADS)
    k1_all = jnp.concatenate([k1_c, k1], axis=1)
    k2_all = jnp.concatenate([k2_c, k2], axis=1)
    vc_all = jnp.concatenate([vc_c, vc], axis=1)
    scale_c = DIFF_D ** -0.5
    yc = sweep_query_blocks(lambda a, bq: diff_attend(a, bq, k1_all, k2_all, vc_all, lam, scale_c), q1, q2)
    yc = rms_norm(yc, g_sub) * (1 - lam_init)

    y = gated_merge(h, ya, yb, yc, w_br_a, w_br_b, w_br_c, w_gate, b_gate, w_out)
    if not with_ctx_out:
        return y, None
    ya_c = softmax_attend(qa_c, ka_c, va_c, HEAD_DIM ** -0.5)
    yb_c = softmax_attend(qb_c, kb_c, vb_c, scale_b)
    yc_c = rms_norm(diff_attend(q1_c, q2_c, k1_c, k2_c, vc_c, lam, scale_c), g_sub) * (1 - lam_init)
    y_c = gated_merge(hc, ya_c, yb_c, yc_c, w_br_a, w_br_b, w_br_c, w_gate, b_gate, w_out)
    return y, y_c


def expert_choice_ffn(h, w_router, w_g, w_u, w_d):
    b, t, _ = h.shape
    cap = max(1, EC_CAPACITY_FACTOR * t // N_EXPERTS)
    aff = jax.nn.softmax((h @ w_router).astype(jnp.float32), axis=-1)
    gate_w, idx = lax.top_k(jnp.swapaxes(aff, 1, 2), cap)
    bidx = jnp.arange(b, dtype=jnp.int32)[:, None, None]
    x_sel = jnp.moveaxis(h[bidx, idx], 1, 0)

    def expert(args):
        xe, wg, wu, wd = args
        return (jax.nn.silu(xe @ wg) * (xe @ wu)) @ wd

    y = jnp.moveaxis(lax.map(expert, (x_sel, w_g, w_u, w_d)), 0, 1)
    y = y * gate_w[..., None].astype(y.dtype)
    return jnp.zeros_like(h).at[bidx, idx].add(y)


def setup_inputs(seed: int = 0) -> dict:
    key = jax.random.key(seed)
    ks = jax.random.split(key, 31)
    L, D, E, F = DEPTH, D_MODEL, N_EXPERTS, EXPERT_FF

    def nrm(i, shape, s):
        return jax.random.normal(ks[i], shape, jnp.float32) * s

    return {
        'x': nrm(0, (BATCH, SEQ, D), 1.0),
        'c': nrm(1, (BATCH, D), 1.0),
        'ctx': nrm(2, (BATCH, CTX_LEN, D), 1.0),
        'c_ctx': nrm(3, (D,), 1.0),
        'w_ada': nrm(4, (L, D, 6 * D), 0.5 * D ** -0.5),
        'b_ada': nrm(5, (L, 6 * D), 0.02),
        'w_in': nrm(6, (L, D, IN_DIM), D ** -0.5),
        'na_rpb': nrm(7, (L, NA_HEADS, 2 * NA_WIN_ROWS - 1, 2 * NA_WIN_COLS - 1), 0.1),
        'mla_g_q': 1.0 + nrm(8, (L, MLA_Q_RANK), 0.02),
        'mla_g_kv': 1.0 + nrm(9, (L, MLA_KV_RANK), 0.02),
        'mla_w_uq': nrm(10, (L, MLA_Q_RANK, MLA_HEADS * (MLA_NOPE + MLA_ROPE)), MLA_Q_RANK ** -0.5),
        'mla_w_ukv': nrm(11, (L, MLA_KV_RANK, MLA_HEADS * (MLA_NOPE + MLA_V)), MLA_KV_RANK ** -0.5),
        'diff_lq1': nrm(12, (L, DIFF_D), 0.1),
        'diff_lk1': nrm(13, (L, DIFF_D), 0.1),
        'diff_lq2': nrm(14, (L, DIFF_D), 0.1),
        'diff_lk2': nrm(15, (L, DIFF_D), 0.1),
        'diff_g_sub': 1.0 + nrm(16, (L, DIFF_V), 0.02),
        'w_br_a': nrm(17, (L, A_W, D), DEEPNORM_BETA * A_W ** -0.5),
        'w_br_b': nrm(18, (L, B_W, D), DEEPNORM_BETA * B_W ** -0.5),
        'w_br_c': nrm(19, (L, C_W, D), DEEPNORM_BETA * C_W ** -0.5),
        'w_gate': nrm(20, (L, D, N_BRANCH * D), D ** -0.5),
        'b_gate': nrm(21, (L, N_BRANCH * D), 0.02),
        'w_out': nrm(22, (L, D, D), DEEPNORM_BETA * D ** -0.5),
        'ln1_g': 1.0 + nrm(23, (L, D), 0.02),
        'ln1_b': nrm(24, (L, D), 0.02),
        'w_router': nrm(25, (L, D, E), D ** -0.5),
        'w_exp_gate': nrm(26, (L, E, D, F), D ** -0.5),
        'w_exp_up': nrm(27, (L, E, D, F), D ** -0.5),
        'w_exp_down': nrm(28, (L, E, F, D), DEEPNORM_BETA * F ** -0.5),
        'ln2_g': 1.0 + nrm(29, (L, D), 0.02),
        'ln2_b': nrm(30, (L, D), 0.02),
    }


def reference(x, c, ctx, c_ctx, w_ada, b_ada, w_in, na_rpb, mla_g_q, mla_g_kv, mla_w_uq, mla_w_ukv,
              diff_lq1, diff_lk1, diff_lq2, diff_lk2, diff_g_sub, w_br_a, w_br_b, w_br_c,
              w_gate, b_gate, w_out, ln1_g, ln1_b, w_router, w_exp_gate, w_exp_up, w_exp_down,
              ln2_g, ln2_b):
    for l in range(DEPTH):
        last = l == DEPTH - 1
        lam_init = 0.8 - 0.6 * math.exp(-0.3 * l)
        mod = jax.nn.silu(c) @ w_ada[l] + b_ada[l]
        mod_c = jax.nn.silu(c_ctx) @ w_ada[l] + b_ada[l]
        sh1, sc1, g1, sh2, sc2, g2 = jnp.split(mod[:, None, :], 6, axis=-1)
        csh1, csc1, cg1, csh2, csc2, cg2 = jnp.split(mod_c, 6, axis=-1)

        h = modulate(x, sh1, sc1)
        hc = modulate(ctx, csh1, csc1)
        y, y_c = token_mixer(h, hc, lam_init, w_in[l], na_rpb[l], mla_g_q[l], mla_g_kv[l],
                             mla_w_uq[l], mla_w_ukv[l], diff_lq1[l], diff_lk1[l], diff_lq2[l],
                             diff_lk2[l], diff_g_sub[l], w_br_a[l], w_br_b[l], w_br_c[l],
                             w_gate[l], b_gate[l], w_out[l], not last)
        x = layer_norm(DEEPNORM_ALPHA * x + g1 * y, ln1_g[l], ln1_b[l])

        h2 = modulate(x, sh2, sc2)
        f = expert_choice_ffn(h2, w_router[l], w_exp_gate[l], w_exp_up[l], w_exp_down[l])
        x = layer_norm(DEEPNORM_ALPHA * x + g2 * f, ln2_g[l], ln2_b[l])

        if not last:
            ctx = layer_norm(DEEPNORM_ALPHA * ctx + cg1 * y_c, ln1_g[l], ln1_b[l])
            hc2 = modulate(ctx, csh2, csc2)
            f_c = expert_choice_ffn(hc2, w_router[l], w_exp_gate[l], w_exp_up[l], w_exp_down[l])
            ctx = layer_norm(DEEPNORM_ALPHA * ctx + cg2 * f_c, ln2_g[l], ln2_b[l])
    return x
```

```python
import functools
import math

import jax
import jax.numpy as jnp
import numpy as np
from jax import lax
from jax.experimental import pallas as pl
from jax.experimental.pallas import tpu as pltpu

F32 = jnp.float32
BF16 = jnp.bfloat16
I32 = jnp.int32

GRID_W = 64
HEAD_DIM = 64
NA_HEADS = 8
NA_WIN_ROWS = 8
NA_WIN_COLS = 16
MLA_HEADS = 8
MLA_Q_RANK = 256
MLA_KV_RANK = 128
MLA_NOPE = 64
MLA_ROPE = 32
MLA_V = 64
DIFF_HEADS = 4
DIFF_D = 64
DIFF_V = 2 * DIFF_D
A_W = NA_HEADS * HEAD_DIM
C_QK = DIFF_HEADS * 2 * DIFF_D
C_W = DIFF_HEADS * DIFF_V
N_BRANCH = 3
N_EXPERTS = 16
EC_CAPACITY_FACTOR = 2
ROPE_BASE = 10000.0
EPS = 1e-6

LANE = 128
HEAD_PAD = 128
VMEM_LIMIT = 56 * 1024 * 1024
SLOT_CHUNK = 256
TOKEN_TILE = 512

OFF_QA, OFF_KA, OFF_VA = 0, 512, 1024
OFF_CQ = 1536
OFF_CKV = 1792
OFF_KR, OFF_KRS = 1920, 2048
OFF_DQ, OFF_DQS = 2176, 2688
OFF_DK, OFF_DKS = 3200, 3712
OFF_DV = 4224
W_EXT = 4736


def _cparams(sem, vmem=VMEM_LIMIT):
    return pltpu.CompilerParams(dimension_semantics=sem, vmem_limit_bytes=vmem)


def _ln_core(x):
    mu = jnp.mean(x, axis=-1, keepdims=True)
    xc = x - mu
    var = jnp.mean(xc * xc, axis=-1, keepdims=True)
    return xc * lax.rsqrt(var + EPS)


def _dot(a, b):
    return jnp.dot(a, b, preferred_element_type=F32)


def _dot_nt(a, b):
    return lax.dot_general(a, b, (((1,), (1,)), ((), ())), preferred_element_type=F32)


def _ada_kernel(c_ref, w_ref, b_ref, o_ref):
    c = c_ref[...]
    s = c * jax.nn.sigmoid(c)
    o_ref[...] = jnp.dot(s, w_ref[...], precision=lax.Precision.HIGHEST,
                         preferred_element_type=F32) + b_ref[...]


def _ada_call(cvec, w_ada, b_ada):
    nl, d, d6 = w_ada.shape
    tn = _pick_tile(d6, (1024, 768, 512, 256, 128))
    return pl.pallas_call(
        _ada_kernel,
        grid=(nl, d6 // tn),
        in_specs=[pl.BlockSpec((8, d), lambda l, j: (0, 0)),
                  pl.BlockSpec((None, d, tn), lambda l, j: (l, 0, j)),
                  pl.BlockSpec((None, 1, tn), lambda l, j: (l, 0, j))],
        out_specs=pl.BlockSpec((None, 8, tn), lambda l, j: (l, 0, j)),
        out_shape=jax.ShapeDtypeStruct((nl, 8, d6), F32),
        compiler_params=_cparams(("parallel", "parallel")),
        name="ada_mod",
    )(cvec, w_ada, b_ada.reshape(nl, 1, d6))


def _inproj_kernel(x_ref, sh_ref, sc_ref, w_ref, gq_ref, gkv_ref, wuq1_ref, wuq2_ref,
                   wkvk_ref, wkvv_ref, cm_ref, sm_ref, cd_ref, sd_ref,
                   qa_ref, ka_ref, va_ref, qm_ref, km_ref, vm_ref, dq_ref, dk_ref, dv_ref,
                   *, scale_a, scale_b, scale_c):
    h = _ln_core(x_ref[...]) * (1.0 + sc_ref[...]) + sh_ref[...]
    hb = h.astype(BF16)

    def seg(off, width):
        return _dot(hb, w_ref[:, off:off + width])

    qa_ref[...] = (seg(OFF_QA, A_W) * scale_a).astype(BF16)
    ka_ref[...] = seg(OFF_KA, A_W).astype(BF16)
    va_ref[...] = seg(OFF_VA, A_W).astype(BF16)

    cm = cm_ref[...]
    sm = sm_ref[...]
    cd = cd_ref[...]
    sd = sd_ref[...]

    cq = seg(OFF_CQ, MLA_Q_RANK)
    cqn = (cq * lax.rsqrt(jnp.mean(cq * cq, axis=-1, keepdims=True) + EPS) * gq_ref[...]).astype(BF16)
    q1 = _dot(cqn, wuq1_ref[...])
    q2 = _dot(cqn, wuq2_ref[...])
    ckv = seg(OFF_CKV, MLA_KV_RANK)
    ckvn = (ckv * lax.rsqrt(jnp.mean(ckv * ckv, axis=-1, keepdims=True) + EPS) * gkv_ref[...]).astype(BF16)
    kn = _dot(ckvn, wkvk_ref[...])
    vm_ref[...] = _dot(ckvn, wkvv_ref[...]).astype(BF16)
    kr = seg(OFF_KR, HEAD_PAD) * cm + seg(OFF_KRS, HEAD_PAD) * sm
    for hh in range(MLA_HEADS):
        sl = slice(hh * HEAD_PAD, (hh + 1) * HEAD_PAD)
        qm_ref[:, sl] = ((q1[:, sl] * cm + q2[:, sl] * sm) * scale_b).astype(BF16)
        km_ref[:, sl] = (kn[:, sl] + kr).astype(BF16)

    dq = seg(OFF_DQ, C_QK)
    dqs = seg(OFF_DQS, C_QK)
    dk = seg(OFF_DK, C_QK)
    dks = seg(OFF_DKS, C_QK)
    for hh in range(DIFF_HEADS):
        sl = slice(hh * HEAD_PAD, (hh + 1) * HEAD_PAD)
        dq_ref[:, sl] = ((dq[:, sl] * cd + dqs[:, sl] * sd) * scale_c).astype(BF16)
        dk_ref[:, sl] = (dk[:, sl] * cd + dks[:, sl] * sd).astype(BF16)
    dv_ref[...] = seg(OFF_DV, C_W).astype(BF16)


def _inproj_call(x, sh, sc, wp, tabs, tm):
    b, n, d = x.shape
    nt = n // tm
    cm, sm, cd, sd = tabs
    row = lambda w: pl.BlockSpec((None, tm, w), lambda i, bb: (bb, i, 0))
    vec = lambda w: pl.BlockSpec((None, 1, w), lambda i, bb: (bb, 0, 0))
    full = lambda a: pl.BlockSpec(a.shape, lambda i, bb: (0,) * a.ndim)
    tab = pl.BlockSpec((tm, HEAD_PAD), lambda i, bb: (i, 0))
    outs = [A_W, A_W, A_W, MLA_HEADS * HEAD_PAD, MLA_HEADS * HEAD_PAD, MLA_HEADS * HEAD_PAD, C_QK, C_QK, C_W]
    kern = functools.partial(_inproj_kernel, scale_a=HEAD_DIM ** -0.5,
                             scale_b=(MLA_NOPE + MLA_ROPE) ** -0.5, scale_c=DIFF_D ** -0.5)
    return pl.pallas_call(
        kern,
        grid=(nt, b),
        in_specs=[row(d), vec(d), vec(d), full(wp["w_in"]), full(wp["g_q"]), full(wp["g_kv"]),
                  full(wp["wuq1"]), full(wp["wuq2"]), full(wp["wkvk"]), full(wp["wkvv"]),
                  tab, tab, tab, tab],
        out_specs=[row(w) for w in outs],
        out_shape=[jax.ShapeDtypeStruct((b, n, w), BF16) for w in outs],
        compiler_params=_cparams(("parallel", "parallel")),
        name="inproj",
    )(x, sh, sc, wp["w_in"], wp["g_q"], wp["g_kv"], wp["wuq1"], wp["wuq2"], wp["wkvk"], wp["wkvv"],
      cm, sm, cd, sd)


def _na_kernel(q_ref, kw_ref, vw_ref, kc_ref, vc_ref, bias_ref, o_ref):
    q = q_ref[...]
    kw = kw_ref[0]
    vw = vw_ref[0]
    kc = kc_ref[...]
    vc = vc_ref[...]
    outs = []
    for hh in range(NA_HEADS):
        sl = slice(hh * HEAD_DIM, (hh + 1) * HEAD_DIM)
        qh = q[:, sl]
        s_w = _dot_nt(qh, kw[:, sl]) + bias_ref[hh]
        s_c = _dot_nt(qh, kc[:, sl])
        m = jnp.maximum(jnp.max(s_w, axis=-1, keepdims=True), jnp.max(s_c, axis=-1, keepdims=True))
        p_w = jnp.exp(s_w - m)
        p_c = jnp.exp(s_c - m)
        l = jnp.sum(p_w, axis=-1, keepdims=True) + jnp.sum(p_c, axis=-1, keepdims=True)
        o = _dot(p_w.astype(BF16), vw[:, sl]) + _dot(p_c.astype(BF16), vc[:, sl])
        outs.append(o / l)
    o_ref[...] = jnp.concatenate(outs, axis=1).astype(BF16)


def _na_call(qa, ka, va, kac, vac, bias_tab):
    b, n, aw = qa.shape
    rows = n // GRID_W
    kr = NA_WIN_ROWS
    assert rows >= kr
    lctx = kac.shape[1]
    win = kr * GRID_W

    def rstart(r):
        return jnp.clip(r - kr // 2, 0, rows - kr)

    win_spec = pl.BlockSpec((pl.Element(1), pl.Element(win), pl.Element(aw)),
                            lambda bb, r: (bb, rstart(r) * GRID_W, 0))
    ctx_spec = pl.BlockSpec((None, lctx, aw), lambda bb, r: (bb, 0, 0))
    return pl.pallas_call(
        _na_kernel,
        grid=(b, rows),
        in_specs=[pl.BlockSpec((None, GRID_W, aw), lambda bb, r: (bb, r, 0)),
                  win_spec, win_spec, ctx_spec, ctx_spec,
                  pl.BlockSpec((None, NA_HEADS, GRID_W, win), lambda bb, r: (r - rstart(r), 0, 0, 0))],
        out_specs=pl.BlockSpec((None, GRID_W, aw), lambda bb, r: (bb, r, 0)),
        out_shape=jax.ShapeDtypeStruct((b, n, aw), BF16),
        compiler_params=_cparams(("parallel", "arbitrary")),
        name="na_attn",
    )(qa, ka, va, kac, vac, bias_tab)


def _na_bias_table(rpb):
    kr, kc = NA_WIN_ROWS, NA_WIN_COLS
    c_idx = np.arange(GRID_W)
    c_start = np.clip(c_idx - kc // 2, 0, GRID_W - kc)
    kcol = np.arange(GRID_W)
    inside = (kcol[None, :] >= c_start[:, None]) & (kcol[None, :] < c_start[:, None] + kc)
    col_off = np.clip(kcol[None, :] - c_idx[:, None] + (NA_WIN_COLS - 1), 0, 2 * NA_WIN_COLS - 2)
    cls = np.arange(kr)[:, None]
    j = np.arange(kr)[None, :]
    row_off = j - cls + (NA_WIN_ROWS - 1)
    t = rpb[:, row_off][:, :, :, col_off]
    t = jnp.where(jnp.asarray(inside)[None, None, None], t, -1e30)
    t = jnp.transpose(t, (1, 0, 3, 2, 4))
    return t.reshape(kr, rpb.shape[0], GRID_W, kr * GRID_W).astype(F32)


def _flash_kernel(*refs, tq, tk, nk_main, diff, lam_init):
    it = iter(refs)
    q_ref = next(it)
    kc_ref = next(it)
    vc_ref = next(it)
    k_ref = v_ref = None
    if nk_main:
        k_ref = next(it)
        v_ref = next(it)
    if diff:
        lq1_ref, lk1_ref, lq2_ref, lk2_ref, gsub_ref = (next(it) for _ in range(5))
    o_ref = next(it)
    m_ref, l_ref, acc_ref = next(it), next(it), next(it)

    q = q_ref[...]
    if diff:
        lane = lax.broadcasted_iota(I32, q.shape, 1)
        zero = jnp.zeros_like(q)
        q = jnp.concatenate([jnp.where(lane < DIFF_D, q, zero), jnp.where(lane >= DIFF_D, q, zero)], axis=0)

    def block(kb, vb, first):
        s = _dot_nt(q, kb)
        smax = jnp.max(s, axis=-1, keepdims=True)
        if first:
            m_new = smax
            p = jnp.exp(s - m_new)
            l_ref[...] = jnp.sum(p, axis=-1, keepdims=True)
            acc_ref[...] = _dot(p.astype(BF16), vb)
        else:
            m_old = m_ref[...]
            m_new = jnp.maximum(m_old, smax)
            alpha = jnp.exp(m_old - m_new)
            p = jnp.exp(s - m_new)
            l_ref[...] = alpha * l_ref[...] + jnp.sum(p, axis=-1, keepdims=True)
            acc_ref[...] = alpha * acc_ref[...] + _dot(p.astype(BF16), vb)
        m_ref[...] = m_new

    block(kc_ref[...], vc_ref[...], True)
    if nk_main:
        def body(j, carry):
            off = pl.multiple_of(j * tk, tk)
            block(k_ref[pl.ds(off, tk), :], v_ref[pl.ds(off, tk), :], False)
            return carry
        lax.fori_loop(0, nk_main // tk, body, 0)

    o = acc_ref[...] / l_ref[...]
    if diff:
        lam = (jnp.exp(jnp.sum(lq1_ref[...] * lk1_ref[...], axis=-1, keepdims=True))
               - jnp.exp(jnp.sum(lq2_ref[...] * lk2_ref[...], axis=-1, keepdims=True)) + lam_init)
        o = o[:tq] - lam * o[tq:]
        o = o * lax.rsqrt(jnp.mean(o * o, axis=-1, keepdims=True) + EPS) * gsub_ref[...] * (1.0 - lam_init)
    o_ref[...] = o.astype(BF16)


def _pick_tile(n, cands):
    for c in cands:
        if n % c == 0:
            return c
    return n


def _flash_call(q, kc, vc, k, v, *, tq, diff=False, lam_params=None, lam_init=0.0, name="flash"):
    b, n, hw = q.shape
    nh = hw // HEAD_PAD
    lctx = kc.shape[1]
    nk_main = 0 if k is None else k.shape[1]
    tq = min(tq, n)
    tk = _pick_tile(nk_main, (1024, 512, 256, 128)) if nk_main else 0
    m_rows = 2 * tq if diff else tq
    qspec = pl.BlockSpec((None, tq, HEAD_PAD), lambda bb, hh, i: (bb, i, hh))
    cspec = pl.BlockSpec((None, lctx, HEAD_PAD), lambda bb, hh, i: (bb, 0, hh))
    in_specs = [qspec, cspec, cspec]
    args = [q, kc, vc]
    if nk_main:
        kspec = pl.BlockSpec((None, nk_main, HEAD_PAD), lambda bb, hh, i: (bb, 0, hh))
        in_specs += [kspec, kspec]
        args += [k, v]
    if diff:
        small = lambda a: pl.BlockSpec(a.shape, lambda bb, hh, i: (0,) * a.ndim)
        in_specs += [small(a) for a in lam_params]
        args += list(lam_params)
    kern = functools.partial(_flash_kernel, tq=tq, tk=tk, nk_main=nk_main, diff=diff, lam_init=lam_init)
    return pl.pallas_call(
        kern,
        grid=(b, nh, n // tq),
        in_specs=in_specs,
        out_specs=qspec,
        out_shape=jax.ShapeDtypeStruct((b, n, hw), BF16),
        scratch_shapes=[pltpu.VMEM((m_rows, 1), F32), pltpu.VMEM((m_rows, 1), F32),
                        pltpu.VMEM((m_rows, HEAD_PAD), F32)],
        compiler_params=_cparams(("parallel", "parallel", "arbitrary")),
        name=name,
    )(*args)


def _merge_kernel(x_ref, sh1_ref, sc1_ref, g1_ref, sh2_ref, sc2_ref, ya_ref, yb_ref, yc_ref,
                  wg_ref, bg_ref, wa_ref, wb_ref, wc_ref, wo_ref, lng_ref, lnb_ref, wr_ref,
                  x1_ref, h2_ref, aff_ref, *, alpha):
    x = x_ref[...]
    d = x.shape[-1]
    hb = (_ln_core(x) * (1.0 + sc1_ref[...]) + sh1_ref[...]).astype(BF16)

    def gate(i):
        return jax.nn.sigmoid(_dot(hb, wg_ref[:, i * d:(i + 1) * d]) + bg_ref[:, i * d:(i + 1) * d])

    m = gate(0) * _dot(ya_ref[...], wa_ref[...])
    m = m + gate(1) * _dot(yb_ref[...], wb_ref[...])
    m = m + gate(2) * _dot(yc_ref[...], wc_ref[...])
    y = _dot(m.astype(BF16), wo_ref[...])
    x1 = _ln_core(alpha * x + g1_ref[...] * y) * lng_ref[...] + lnb_ref[...]
    x1_ref[...] = x1
    h2 = _ln_core(x1) * (1.0 + sc2_ref[...]) + sh2_ref[...]
    h2_ref[...] = h2.astype(BF16)
    logits = jnp.dot(h2, wr_ref[...], precision=lax.Precision.HIGHEST, preferred_element_type=F32)
    lane = lax.broadcasted_iota(I32, logits.shape, 1)
    logits = jnp.where(lane < N_EXPERTS, logits, -1e30)
    e = jnp.exp(logits - jnp.max(logits, axis=-1, keepdims=True))
    aff = e / jnp.sum(e, axis=-1, keepdims=True)
    aff_ref[...] = aff[:, :N_EXPERTS]


def _merge_call(x, mods, ya, yb, yc, wp, alpha, tm):
    b, n, d = x.shape
    tm = min(tm, n)
    row = lambda w: pl.BlockSpec((None, tm, w), lambda bb, i: (bb, i, 0))
    vec = pl.BlockSpec((None, 1, d), lambda bb, i: (bb, 0, 0))
    full = lambda a: pl.BlockSpec(a.shape, lambda bb, i: (0,) * a.ndim)
    ws = [wp["w_gate"], wp["b_gate"], wp["w_br_a"], wp["w_br_b"], wp["w_br_c"], wp["w_out"],
          wp["ln1_g"], wp["ln1_b"], wp["w_router"]]
    return pl.pallas_call(
        functools.partial(_merge_kernel, alpha=alpha),
        grid=(b, n // tm),
        in_specs=[row(d)] + [vec] * 5 + [row(ya.shape[-1]), row(yb.shape[-1]), row(yc.shape[-1])]
        + [full(a) for a in ws],
        out_specs=[row(d), row(d), row(N_EXPERTS)],
        out_shape=[jax.ShapeDtypeStruct((b, n, d), F32), jax.ShapeDtypeStruct((b, n, d), BF16),
                   jax.ShapeDtypeStruct((b, n, N_EXPERTS), F32)],
        compiler_params=_cparams(("parallel", "parallel")),
        name="merge",
    )(x, *mods, ya, yb, yc, *ws)


def _topk_kernel(a_ref, posm_ref, pos_ref, *, cap):
    nck = a_ref.shape[0]

    def count(pred):
        bits = lax.bitcast_convert_type(a_ref[...], I32)
        c = jnp.sum(pred(bits).astype(F32), axis=0)
        return jnp.sum(c, axis=-1, keepdims=True)

    def bit_step(i, t):
        cand = t | jnp.left_shift(jnp.int32(1), 30 - i)
        return jnp.where(count(lambda bits: bits >= cand[None]) >= cap, cand, t)

    t = lax.fori_loop(0, 31, bit_step, jnp.zeros((N_EXPERTS, 1), I32))
    need = cap - count(lambda bits: bits > t[None])

    r = lax.broadcasted_iota(I32, (LANE, LANE), 0)
    c = lax.broadcasted_iota(I32, (LANE, LANE), 1)
    u_incl = (r <= c).astype(BF16)
    u_excl = (r < c).astype(BF16)

    def chunk_step(ci, carry):
        run_eq, run_sel = carry
        bc = lax.bitcast_convert_type(a_ref[ci], I32)
        gt = bc > t
        eq = bc == t
        eq_before = _dot(eq.astype(BF16), u_excl) + run_eq
        sel = gt | (eq & (eq_before < need))
        sel_incl = (_dot(sel.astype(BF16), u_incl) + run_sel).astype(I32)
        pos_ref[ci] = sel_incl
        posm_ref[ci] = jnp.where(sel, sel_incl - 1, -1)
        run_eq = run_eq + jnp.sum(eq.astype(F32), axis=-1, keepdims=True)
        run_sel = run_sel + jnp.sum(sel.astype(F32), axis=-1, keepdims=True)
        return run_eq, run_sel

    z = jnp.zeros((N_EXPERTS, 1), F32)
    lax.fori_loop(0, nck, chunk_step, (z, z))


def _topk_call(aff_ck, cap):
    b, nck, e, _ = aff_ck.shape
    spec = pl.BlockSpec((None, nck, e, LANE), lambda bb: (bb, 0, 0, 0))
    return pl.pallas_call(
        functools.partial(_topk_kernel, cap=cap),
        grid=(b,),
        in_specs=[spec],
        out_specs=[spec, spec],
        out_shape=[jax.ShapeDtypeStruct(aff_ck.shape, I32)] * 2,
        compiler_params=_cparams(("parallel",)),
        name="topk",
    )(aff_ck)


def _gather_kernel(chunk_ref, tile_ref, valid_ref, first_ref, posm_ref, h_ref, o_ref, acc_ref, *, cj, steps):
    bb, e, s = pl.program_id(0), pl.program_id(1), pl.program_id(2)
    idx = (bb * pl.num_programs(1) + e) * steps + s

    @pl.when(first_ref[idx] == 1)
    def _():
        acc_ref[...] = jnp.zeros_like(acc_ref)

    @pl.when(valid_ref[idx] == 1)
    def _():
        tt = h_ref.shape[0]
        slot = chunk_ref[idx] * cj + lax.broadcasted_iota(I32, (cj, tt), 0)
        onehot = (posm_ref[...] == slot).astype(BF16)
        acc_ref[...] += _dot(onehot, h_ref[...])

    o_ref[...] = acc_ref[...].astype(BF16)


def _gather_call(wl, posm_rows, h2, cap, cj, tt):
    b, n, d = h2.shape
    e = N_EXPERTS
    steps = wl["g_chunk"].shape[0] // (b * e)
    grid_spec = pltpu.PrefetchScalarGridSpec(
        num_scalar_prefetch=4,
        grid=(b, e, steps),
        in_specs=[
            pl.BlockSpec((None, None, None, 1, tt),
                         lambda bb, ee, s, ch, tl, va, fi: (bb, ee, tl[(bb * e + ee) * steps + s], 0, 0)),
            pl.BlockSpec((None, tt, d), lambda bb, ee, s, ch, tl, va, fi: (bb, tl[(bb * e + ee) * steps + s], 0)),
        ],
        out_specs=pl.BlockSpec((None, None, cj, d),
                               lambda bb, ee, s, ch, tl, va, fi: (bb, ee, ch[(bb * e + ee) * steps + s], 0)),
        scratch_shapes=[pltpu.VMEM((cj, d), F32)],
    )
    return pl.pallas_call(
        functools.partial(_gather_kernel, cj=cj, steps=steps),
        grid_spec=grid_spec,
        out_shape=jax.ShapeDtypeStruct((b, e, cap, d), BF16),
        compiler_params=_cparams(("arbitrary", "arbitrary", "arbitrary")),
        name="moe_gather",
    )(wl["g_chunk"], wl["g_tile"], wl["g_valid"], wl["g_first"], posm_rows, h2)


def _ffn_kernel(x_ref, wg_ref, wu_ref, wd_ref, o_ref, acc_ref):
    f = pl.program_id(2)

    @pl.when(f == 0)
    def _():
        acc_ref[...] = jnp.zeros_like(acc_ref)

    x = x_ref[...]
    a = _dot(x, wg_ref[...].astype(BF16))
    u = _dot(x, wu_ref[...].astype(BF16))
    hmid = (a * jax.nn.sigmoid(a) * u).astype(BF16)
    acc_ref[...] += _dot(hmid, wd_ref[...].astype(BF16))

    @pl.when(f == pl.num_programs(2) - 1)
    def _():
        o_ref[...] = acc_ref[...].astype(BF16)


def _ffn_call(xsel, w_g, w_u, w_d):
    b, e, cap, d = xsel.shape
    ff = w_g.shape[-1]
    tf = _pick_tile(ff, (256, 128))
    xspec = pl.BlockSpec((None, None, cap, d), lambda ee, bb, f: (bb, ee, 0, 0))
    return pl.pallas_call(
        _ffn_kernel,
        grid=(e, b, ff // tf),
        in_specs=[xspec,
                  pl.BlockSpec((None, d, tf), lambda ee, bb, f: (ee, 0, f)),
                  pl.BlockSpec((None, d, tf), lambda ee, bb, f: (ee, 0, f)),
                  pl.BlockSpec((None, tf, d), lambda ee, bb, f: (ee, f, 0))],
        out_specs=xspec,
        out_shape=jax.ShapeDtypeStruct(xsel.shape, BF16),
        scratch_shapes=[pltpu.VMEM((cap, d), F32)],
        compiler_params=_cparams(("parallel", "parallel", "arbitrary")),
        name="moe_ffn",
    )(xsel, w_g, w_u, w_d)


def _combine_kernel(tile_ref, exp_ref, chunk_ref, valid_ref, first_ref, last_ref,
                    posm_ref, aff_ref, y_ref, x_ref, g2_ref, lng_ref, lnb_ref, o_ref, acc_ref,
                    *, cj, steps, alpha):
    idx = pl.program_id(0) * steps + pl.program_id(1)

    @pl.when(first_ref[idx] == 1)
    def _():
        acc_ref[...] = jnp.zeros_like(acc_ref)

    @pl.when(valid_ref[idx] == 1)
    def _():
        tt = acc_ref.shape[0]
        lane = lax.broadcasted_iota(I32, (tt, N_EXPERTS), 1)
        pick = lane == exp_ref[idx]
        slot = jnp.sum(jnp.where(pick, posm_ref[...].astype(F32), 0.0), axis=-1, keepdims=True)
        gate = jnp.sum(jnp.where(pick, aff_ref[...], 0.0), axis=-1, keepdims=True)
        cols = chunk_ref[idx] * cj + lax.broadcasted_iota(I32, (tt, cj), 1)
        onehot = (slot == cols.astype(F32)).astype(BF16)
        acc_ref[...] += gate * _dot(onehot, y_ref[...])

    @pl.when(last_ref[idx] == 1)
    def _():
        o_ref[...] = _ln_core(alpha * x_ref[...] + g2_ref[...] * acc_ref[...]) * lng_ref[...] + lnb_ref[...]


def _combine_call(wl, posm_tm, aff_tm, y, x1, g2, ln_g, ln_b, alpha, cj, tt):
    b, n, d = x1.shape
    e = N_EXPERTS
    steps = wl["c_tile"].shape[0] // b
    tok = lambda w: pl.BlockSpec((None, tt, w), lambda bb, s, tl, ex, ch, va, fi, la: (bb, tl[bb * steps + s], 0))
    small = lambda a: pl.BlockSpec(a.shape, lambda bb, s, tl, ex, ch, va, fi, la: (0,) * a.ndim)
    grid_spec = pltpu.PrefetchScalarGridSpec(
        num_scalar_prefetch=6,
        grid=(b, steps),
        in_specs=[tok(e), tok(e),
                  pl.BlockSpec((None, None, cj, d),
                               lambda bb, s, tl, ex, ch, va, fi, la: (bb, ex[bb * steps + s], ch[bb * steps + s], 0)),
                  tok(d),
                  pl.BlockSpec((None, 1, d), lambda bb, s, tl, ex, ch, va, fi, la: (bb, 0, 0)),
                  small(ln_g), small(ln_b)],
        out_specs=tok(d),
        scratch_shapes=[pltpu.VMEM((tt, d), F32)],
    )
    return pl.pallas_call(
        functools.partial(_combine_kernel, cj=cj, steps=steps, alpha=alpha),
        grid_spec=grid_spec,
        out_shape=jax.ShapeDtypeStruct((b, n, d), F32),
        compiler_params=_cparams(("arbitrary", "arbitrary")),
        name="moe_combine",
    )(wl["c_tile"], wl["c_exp"], wl["c_chunk"], wl["c_valid"], wl["c_first"], wl["c_last"],
      posm_tm, aff_tm, y, x1, g2, ln_g, ln_b)


def _work_lists(pos_rows, cap, cj, tt):
    b, e, n = pos_rows.shape
    nt, nc = n // tt, cap // cj
    pend = pos_rows[:, :, tt - 1::tt]
    pstart = jnp.concatenate([jnp.zeros((b, e, 1), I32), pend[:, :, :-1]], axis=-1)
    nonempty = pend > pstart
    clo = jnp.minimum(pstart // cj, nc - 1)
    chi = jnp.maximum(pend - 1, 0) // cj
    npairs = jnp.where(nonempty, chi - clo + 1, 0)

    sg = nt + nc
    cum = jnp.cumsum(npairs, axis=-1)
    total = cum[..., -1:]
    s = jnp.arange(sg, dtype=I32)[None, None, :]
    s_eff = jnp.minimum(s, total - 1)
    tile = jnp.sum((cum[..., None, :] <= s_eff[..., :, None]).astype(I32), axis=-1)
    off = jnp.take_along_axis(cum - npairs, tile, axis=-1)
    chunk = jnp.take_along_axis(clo, tile, axis=-1) + (s_eff - off)
    valid = (s < total).astype(I32)
    prev = jnp.concatenate([jnp.full((b, e, 1), -1, I32), chunk[..., :-1]], axis=-1)
    first = (chunk != prev).astype(I32)

    sc = e * (nt + nc)
    nent = jnp.maximum(jnp.transpose(npairs, (0, 2, 1)).reshape(b, nt * e), 1)
    npf = jnp.transpose(npairs, (0, 2, 1)).reshape(b, nt * e)
    clof = jnp.transpose(clo, (0, 2, 1)).reshape(b, nt * e)
    ccum = jnp.cumsum(nent, axis=-1)
    ctotal = ccum[:, -1:]
    s2 = jnp.arange(sc, dtype=I32)[None, :]
    s2_eff = jnp.minimum(s2, ctotal - 1)
    ent = jnp.sum((ccum[:, None, :] <= s2_eff[:, :, None]).astype(I32), axis=-1)
    coff = jnp.take_along_axis(ccum - nent, ent, axis=-1)
    c_tile = ent // e
    c_exp = ent % e
    c_chunk = jnp.minimum(jnp.take_along_axis(clof, ent, axis=-1) + (s2_eff - coff), nc - 1)
    live = s2 < ctotal
    c_valid = (live & (jnp.take_along_axis(npf, ent, axis=-1) > 0)).astype(I32)
    tprev = jnp.concatenate([jnp.full((b, 1), -1, I32), c_tile[:, :-1]], axis=-1)
    tnext = jnp.concatenate([c_tile[:, 1:], jnp.full((b, 1), -1, I32)], axis=-1)
    c_first = (live & (c_tile != tprev)).astype(I32)
    c_last = (live & ((c_tile != tnext) | (s2 == ctotal - 1))).astype(I32)
    flat = lambda a: a.reshape(-1).astype(I32)
    return dict(g_chunk=flat(chunk), g_tile=flat(tile), g_valid=flat(valid), g_first=flat(first),
                c_tile=flat(c_tile), c_exp=flat(c_exp), c_chunk=flat(c_chunk), c_valid=flat(c_valid),
                c_first=flat(c_first), c_last=flat(c_last))


def _moe(h2, aff_tm, x1, g2, ln_g, ln_b, w_g, w_u, w_d, alpha):
    b, n, d = h2.shape
    e = N_EXPERTS
    cap = max(1, EC_CAPACITY_FACTOR * n // e)
    cj = min(SLOT_CHUNK, cap)
    tt = min(TOKEN_TILE, n)
    nck = n // LANE
    aff_ck = jnp.transpose(aff_tm.reshape(b, nck, LANE, e), (0, 1, 3, 2))
    posm_ck, pos_ck = _topk_call(aff_ck, cap)
    posm_rows = jnp.transpose(posm_ck, (0, 2, 1, 3)).reshape(b, e, n)
    pos_rows = jnp.transpose(pos_ck, (0, 2, 1, 3)).reshape(b, e, n)
    posm_tm = jnp.transpose(posm_rows, (0, 2, 1))
    wl = _work_lists(pos_rows, cap, cj, tt)
    xsel = _gather_call(wl, posm_rows.reshape(b, e, n // tt, 1, tt), h2, cap, cj, tt)
    y = _ffn_call(xsel, w_g, w_u, w_d)
    return _combine_call(wl, posm_tm, aff_tm, y, x1, g2, ln_g, ln_b, alpha, cj, tt)


def _swap_perm(rot_dim):
    q = rot_dim // 4
    return np.concatenate([np.arange(q, 2 * q), np.arange(0, q), np.arange(3 * q, 4 * q), np.arange(2 * q, 3 * q)])


def _rope_tables(n, rot_dim):
    t = jnp.arange(n, dtype=I32)
    row = (t // GRID_W).astype(F32)
    col = (t % GRID_W).astype(F32)
    m = rot_dim // 2
    inv = ROPE_BASE ** (-jnp.arange(0, m, 2, dtype=F32) / m)
    ar, ac = row[:, None] * inv, col[:, None] * inv
    cos = jnp.concatenate([jnp.cos(ar), jnp.cos(ar), jnp.cos(ac), jnp.cos(ac)], axis=-1)
    sin = jnp.concatenate([-jnp.sin(ar), jnp.sin(ar), -jnp.sin(ac), jnp.sin(ac)], axis=-1)
    return cos, sin


def _head_tables(n, rope):
    if rope:
        cb, sb = _rope_tables(n, MLA_ROPE)
        cdd, sdd = _rope_tables(n, DIFF_D)
    else:
        cb, sb = jnp.ones((n, MLA_ROPE), F32), jnp.zeros((n, MLA_ROPE), F32)
        cdd, sdd = jnp.ones((n, DIFF_D), F32), jnp.zeros((n, DIFF_D), F32)
    pad = HEAD_PAD - MLA_NOPE - MLA_ROPE
    cm = jnp.concatenate([jnp.ones((n, MLA_NOPE), F32), cb, jnp.zeros((n, pad), F32)], axis=-1)
    sm = jnp.concatenate([jnp.zeros((n, MLA_NOPE), F32), sb, jnp.zeros((n, pad), F32)], axis=-1)
    return cm, sm, jnp.concatenate([cdd, cdd], axis=-1), jnp.concatenate([sdd, sdd], axis=-1)


def _prep_layer(w_in, g_q, g_kv, w_uq, w_ukv, w_br_b, w_router):
    d = w_in.shape[0]
    offs = np.cumsum([0, A_W, A_W, A_W, MLA_Q_RANK, MLA_KV_RANK, MLA_ROPE, C_QK, C_QK, C_W])
    seg = lambda i: w_in[:, offs[i]:offs[i + 1]]
    z = lambda w: jnp.zeros((d, w), w_in.dtype)
    sw32 = _swap_perm(MLA_ROPE)
    sw64 = _swap_perm(DIFF_D)
    chunk_sw = np.concatenate([c * DIFF_D + sw64 for c in range(C_QK // DIFF_D)])
    kr = seg(5)
    pad = HEAD_PAD - MLA_NOPE - MLA_ROPE
    w_ext = jnp.concatenate([
        seg(0), seg(1), seg(2), seg(3), seg(4),
        z(MLA_NOPE), kr, z(pad), z(MLA_NOPE), kr[:, sw32], z(pad),
        seg(6), seg(6)[:, chunk_sw], seg(7), seg(7)[:, chunk_sw], seg(8)], axis=1).astype(BF16)
    assert w_ext.shape[1] == W_EXT

    qd = MLA_NOPE + MLA_ROPE
    wq = w_uq.reshape(MLA_Q_RANK, MLA_HEADS, qd)
    zq = lambda w: jnp.zeros((MLA_Q_RANK, MLA_HEADS, w), w_uq.dtype)
    wuq1 = jnp.concatenate([wq, zq(pad)], axis=-1).reshape(MLA_Q_RANK, MLA_HEADS * HEAD_PAD).astype(BF16)
    wuq2 = jnp.concatenate([zq(MLA_NOPE), wq[:, :, MLA_NOPE:][:, :, sw32], zq(pad)],
                           axis=-1).reshape(MLA_Q_RANK, MLA_HEADS * HEAD_PAD).astype(BF16)
    wkv = w_ukv.reshape(MLA_KV_RANK, MLA_HEADS, MLA_NOPE + MLA_V)
    zk = jnp.zeros((MLA_KV_RANK, MLA_HEADS, HEAD_PAD - MLA_NOPE), w_ukv.dtype)
    zv = jnp.zeros((MLA_KV_RANK, MLA_HEADS, HEAD_PAD - MLA_V), w_ukv.dtype)
    wkvk = jnp.concatenate([wkv[:, :, :MLA_NOPE], zk], axis=-1).reshape(MLA_KV_RANK, -1).astype(BF16)
    wkvv = jnp.concatenate([wkv[:, :, MLA_NOPE:], zv], axis=-1).reshape(MLA_KV_RANK, -1).astype(BF16)
    wb = w_br_b.reshape(MLA_HEADS, MLA_V, d)
    wb = jnp.concatenate([wb, jnp.zeros((MLA_HEADS, HEAD_PAD - MLA_V, d), wb.dtype)], axis=1)
    wb = wb.reshape(MLA_HEADS * HEAD_PAD, d).astype(BF16)
    wr = jnp.concatenate([w_router, jnp.zeros((d, LANE - N_EXPERTS), w_router.dtype)], axis=1)
    return dict(w_in=w_ext, g_q=g_q.reshape(1, -1), g_kv=g_kv.reshape(1, -1), wuq1=wuq1, wuq2=wuq2,
                wkvk=wkvk, wkvv=wkvv, w_br_b=wb, w_router=wr)


def _pad_heads(t, nh):
    b, n, w = t.shape
    hd = w // nh
    t = t.reshape(b, n, nh, hd)
    t = jnp.concatenate([t, jnp.zeros((b, n, nh, HEAD_PAD - hd), t.dtype)], axis=-1)
    return t.reshape(b, n, nh * HEAD_PAD)


def _unpad_heads(t, nh, hd):
    b, n, _ = t.shape
    return t.reshape(b, n, nh, HEAD_PAD)[..., :hd].reshape(b, n, nh * hd)


def kernel(x, c, ctx, c_ctx, w_ada, b_ada, w_in, na_rpb, mla_g_q, mla_g_kv, mla_w_uq, mla_w_ukv,
           diff_lq1, diff_lk1, diff_lq2, diff_lk2, diff_g_sub, w_br_a, w_br_b, w_br_c,
           w_gate, b_gate, w_out, ln1_g, ln1_b, w_router, w_exp_gate, w_exp_up, w_exp_down,
           ln2_g, ln2_b):
    depth = w_ada.shape[0]
    b, n, d = x.shape
    lctx = ctx.shape[1]
    alpha = (2 * depth) ** 0.25

    cvec = jnp.concatenate([c, jnp.broadcast_to(c_ctx[None], (8 - b, d))], axis=0)
    mod_all = _ada_call(cvec, w_ada, b_ada)
    tabs_x = _head_tables(n, True)
    tabs_c = _head_tables(lctx, False)

    for l in range(depth):
        last = l == depth - 1
        lam_init = 0.8 - 0.6 * math.exp(-0.3 * l)
        mod = mod_all[l, :b].reshape(b, 1, 6 * d)
        mod_c = jnp.broadcast_to(mod_all[l, b:b + 1].reshape(1, 1, 6 * d), (b, 1, 6 * d))
        mods = [mod[..., i * d:(i + 1) * d] for i in range(6)]
        mods_c = [mod_c[..., i * d:(i + 1) * d] for i in range(6)]
        wp = _prep_layer(w_in[l], mla_g_q[l], mla_g_kv[l], mla_w_uq[l], mla_w_ukv[l], w_br_b[l], w_router[l])
        wp.update(w_gate=w_gate[l].astype(BF16), b_gate=b_gate[l].reshape(1, -1),
                  w_br_a=w_br_a[l].astype(BF16), w_br_c=w_br_c[l].astype(BF16), w_out=w_out[l].astype(BF16),
                  ln1_g=ln1_g[l].reshape(1, -1), ln1_b=ln1_b[l].reshape(1, -1))
        lam_params = [diff_lq1[l].reshape(1, -1), diff_lk1[l].reshape(1, -1), diff_lq2[l].reshape(1, -1),
                      diff_lk2[l].reshape(1, -1), diff_g_sub[l].reshape(1, -1)]
        ln2g, ln2b = ln2_g[l].reshape(1, -1), ln2_b[l].reshape(1, -1)

        zx = _inproj_call(x, mods[0], mods[1], wp, tabs_x, min(512, n))
        zc = _inproj_call(ctx, mods_c[0], mods_c[1], wp, tabs_c, lctx)
        qa, ka, va, qm, km, vm, dq, dk, dv = zx
        qa_c, ka_c, va_c, qm_c, km_c, vm_c, dq_c, dk_c, dv_c = zc

        ya = _na_call(qa, ka, va, ka_c, va_c, _na_bias_table(na_rpb[l]))
        yb = _flash_call(qm, km_c, vm_c, km, vm, tq=512, name="mla_attn")
        yc = _flash_call(dq, dk_c, dv_c, dk, dv, tq=256, diff=True, lam_params=lam_params,
                         lam_init=lam_init, name="diff_attn")
        x1, h2, aff = _merge_call(x, mods[:5], ya, yb, yc, wp, alpha, 512)
        x = _moe(h2, aff, x1, mods[5], ln2g, ln2b, w_exp_gate[l], w_exp_up[l], w_exp_down[l], alpha)

        if not last:
            ya_c = _unpad_heads(
                _flash_call(_pad_heads(qa_c, NA_HEADS), _pad_heads(ka_c, NA_HEADS), _pad_heads(va_c, NA_HEADS),
                            None, None, tq=lctx, name="ctx_na_attn"), NA_HEADS, HEAD_DIM)
            yb_c = _flash_call(qm_c, km_c, vm_c, None, None, tq=lctx, name="ctx_mla_attn")
            yc_c = _flash_call(dq_c, dk_c, dv_c, None, None, tq=lctx, diff=True, lam_params=lam_params,
                               lam_init=lam_init, name="ctx_diff_attn")
            c1, hc2, aff_c = _merge_call(ctx, mods_c[:5], ya_c, yb_c, yc_c, wp, alpha, lctx)
            ctx = _moe(hc2, aff_c, c1, mods_c[5], ln2g, ln2b, w_exp_gate[l], w_exp_up[l], w_exp_down[l], alpha)
    return x
```

```python
import functools
import math

import jax
import jax.numpy as jnp
import numpy as np
from jax import lax
from jax.experimental import pallas as pl
from jax.experimental.pallas import tpu as pltpu

F32 = jnp.float32
BF16 = jnp.bfloat16
I32 = jnp.int32

GRID_W = 64
HEAD_DIM = 64
NA_HEADS = 8
NA_WIN_ROWS = 8
NA_WIN_COLS = 16
MLA_HEADS = 8
MLA_Q_RANK = 256
MLA_KV_RANK = 128
MLA_NOPE = 64
MLA_ROPE = 32
MLA_V = 64
DIFF_HEADS = 4
DIFF_D = 64
DIFF_V = 2 * DIFF_D
A_W = NA_HEADS * HEAD_DIM
C_QK = DIFF_HEADS * 2 * DIFF_D
C_W = DIFF_HEADS * DIFF_V
N_BRANCH = 3
N_EXPERTS = 16
EC_CAPACITY_FACTOR = 2
ROPE_BASE = 10000.0
EPS = 1e-6
LOG2E = math.log2(math.e)

LANE = 128
HEAD_PAD = 128
VMEM_LIMIT = 56 * 1024 * 1024
SLOT_CHUNK = 256
TOKEN_TILE = 512
QUERY_GROUP = 256
INPROJ_TILE = 512

OFF_QA, OFF_KA, OFF_VA = 0, 512, 1024
OFF_CQ = 1536
OFF_CKV = 1792
OFF_KR, OFF_KRS = 1920, 2048
OFF_DQ, OFF_DQS = 2176, 2688
OFF_DK, OFF_DKS = 3200, 3712
OFF_DV = 4224
W_EXT = 4736


def _cparams(sem, vmem=VMEM_LIMIT):
    return pltpu.CompilerParams(dimension_semantics=sem, vmem_limit_bytes=vmem)


def _ln_core(x):
    mu = jnp.mean(x, axis=-1, keepdims=True)
    xc = x - mu
    var = jnp.mean(xc * xc, axis=-1, keepdims=True)
    return xc * lax.rsqrt(var + EPS)


def _dot(a, b):
    return jnp.dot(a, b, preferred_element_type=F32)


def _dot_nt(a, b):
    return lax.dot_general(a, b, (((1,), (1,)), ((), ())), preferred_element_type=F32)


def _ada_kernel(c_ref, w_ref, b_ref, o_ref):
    c = c_ref[...]
    s = c * jax.nn.sigmoid(c)
    o_ref[...] = jnp.dot(s, w_ref[...], precision=lax.Precision.HIGHEST,
                         preferred_element_type=F32) + b_ref[...]


def _ada_call(cvec, w_ada, b_ada):
    nl, d, d6 = w_ada.shape
    tn = _pick_tile(d6, (1024, 768, 512, 256, 128))
    return pl.pallas_call(
        _ada_kernel,
        grid=(nl, d6 // tn),
        in_specs=[pl.BlockSpec((8, d), lambda l, j: (0, 0)),
                  pl.BlockSpec((None, d, tn), lambda l, j: (l, 0, j)),
                  pl.BlockSpec((None, 1, tn), lambda l, j: (l, 0, j))],
        out_specs=pl.BlockSpec((None, 8, tn), lambda l, j: (l, 0, j)),
        out_shape=jax.ShapeDtypeStruct((nl, 8, d6), F32),
        compiler_params=_cparams(("parallel", "parallel")),
        name="ada_mod",
    )(cvec, w_ada, b_ada.reshape(nl, 1, d6))


def _inproj_kernel(x_ref, sh_ref, sc_ref, w_ref, gq_ref, gkv_ref, wuq1_ref, wuq2_ref,
                   wkvk_ref, wkvv_ref, cm_ref, sm_ref, cd_ref, sd_ref,
                   qa_ref, ka_ref, va_ref, qm_ref, km_ref, vm_ref, dq_ref, dk_ref, dv_ref,
                   *, scale_a, scale_b, scale_c):
    h = _ln_core(x_ref[...]) * (1.0 + sc_ref[...]) + sh_ref[...]
    hb = h.astype(BF16)

    def seg(off, width):
        return _dot(hb, w_ref[:, off:off + width])

    qa_ref[...] = (seg(OFF_QA, A_W) * scale_a).astype(BF16)
    ka_ref[...] = seg(OFF_KA, A_W).astype(BF16)
    va_ref[...] = seg(OFF_VA, A_W).astype(BF16)

    cm = cm_ref[...]
    sm = sm_ref[...]
    cd = cd_ref[...]
    sd = sd_ref[...]

    cq = seg(OFF_CQ, MLA_Q_RANK)
    cqn = (cq * lax.rsqrt(jnp.mean(cq * cq, axis=-1, keepdims=True) + EPS) * gq_ref[...]).astype(BF16)
    q1 = _dot(cqn, wuq1_ref[...])
    q2 = _dot(cqn, wuq2_ref[...])
    ckv = seg(OFF_CKV, MLA_KV_RANK)
    ckvn = (ckv * lax.rsqrt(jnp.mean(ckv * ckv, axis=-1, keepdims=True) + EPS) * gkv_ref[...]).astype(BF16)
    kn = _dot(ckvn, wkvk_ref[...])
    vm = _dot(ckvn, wkvv_ref[...])
    kr = seg(OFF_KR, HEAD_PAD) * cm + seg(OFF_KRS, HEAD_PAD) * sm
    for hh in range(MLA_HEADS):
        sl = slice(hh * HEAD_PAD, (hh + 1) * HEAD_PAD)
        qm_ref[hh] = ((q1[:, sl] * cm + q2[:, sl] * sm) * scale_b).T.astype(BF16)
        km_ref[:, sl] = (kn[:, sl] + kr).astype(BF16)
        vm_ref[hh] = vm[:, sl].T.astype(BF16)

    dq = seg(OFF_DQ, C_QK)
    dqs = seg(OFF_DQS, C_QK)
    dk = seg(OFF_DK, C_QK)
    dks = seg(OFF_DKS, C_QK)
    dv = seg(OFF_DV, C_W)
    for hh in range(DIFF_HEADS):
        sl = slice(hh * HEAD_PAD, (hh + 1) * HEAD_PAD)
        dq_ref[hh] = ((dq[:, sl] * cd + dqs[:, sl] * sd) * scale_c).T.astype(BF16)
        dk_ref[:, sl] = (dk[:, sl] * cd + dks[:, sl] * sd).astype(BF16)
        dv_ref[hh] = dv[:, sl].T.astype(BF16)


def _inproj_call(x, sh, sc, wp, tabs, tm):
    b, n, d = x.shape
    nt = n // tm
    cm, sm, cd, sd = tabs
    row = lambda w: pl.BlockSpec((None, tm, w), lambda i, bb: (bb, i, 0))
    vec = lambda w: pl.BlockSpec((None, 1, w), lambda i, bb: (bb, 0, 0))
    full = lambda a: pl.BlockSpec(a.shape, lambda i, bb: (0,) * a.ndim)
    tab = pl.BlockSpec((tm, HEAD_PAD), lambda i, bb: (i, 0))
    qt = lambda nh: (pl.BlockSpec((None, nh, HEAD_PAD, tm), lambda i, bb: (bb, 0, 0, i)),
                     jax.ShapeDtypeStruct((b, nh, HEAD_PAD, n), BF16))
    vt = lambda nh: (pl.BlockSpec((None, nh, None, HEAD_PAD, tm), lambda i, bb: (bb, 0, i, 0, 0)),
                     jax.ShapeDtypeStruct((b, nh, nt, HEAD_PAD, tm), BF16))
    tok = lambda w: (row(w), jax.ShapeDtypeStruct((b, n, w), BF16))
    outs = [tok(A_W), tok(A_W), tok(A_W), qt(MLA_HEADS), tok(MLA_HEADS * HEAD_PAD), vt(MLA_HEADS),
            qt(DIFF_HEADS), tok(C_QK), vt(DIFF_HEADS)]
    kern = functools.partial(_inproj_kernel, scale_a=HEAD_DIM ** -0.5,
                             scale_b=LOG2E * (MLA_NOPE + MLA_ROPE) ** -0.5, scale_c=LOG2E * DIFF_D ** -0.5)
    return pl.pallas_call(
        kern,
        grid=(nt, b),
        in_specs=[row(d), vec(d), vec(d), full(wp["w_in"]), full(wp["g_q"]), full(wp["g_kv"]),
                  full(wp["wuq1"]), full(wp["wuq2"]), full(wp["wkvk"]), full(wp["wkvv"]),
                  tab, tab, tab, tab],
        out_specs=[o[0] for o in outs],
        out_shape=[o[1] for o in outs],
        compiler_params=_cparams(("parallel", "parallel")),
        name="inproj",
    )(x, sh, sc, wp["w_in"], wp["g_q"], wp["g_kv"], wp["wuq1"], wp["wuq2"], wp["wkvk"], wp["wkvv"],
      cm, sm, cd, sd)


def _na_kernel(q_ref, kw_ref, vw_ref, kc_ref, vc_ref, bias_ref, o_ref):
    q = q_ref[...]
    kw = kw_ref[0]
    vw = vw_ref[0]
    kc = kc_ref[...]
    vc = vc_ref[...]
    outs = []
    for hh in range(NA_HEADS):
        sl = slice(hh * HEAD_DIM, (hh + 1) * HEAD_DIM)
        qh = q[:, sl]
        s_w = _dot_nt(qh, kw[:, sl]) + bias_ref[hh]
        s_c = _dot_nt(qh, kc[:, sl])
        m = jnp.maximum(jnp.max(s_w, axis=-1, keepdims=True), jnp.max(s_c, axis=-1, keepdims=True))
        p_w = jnp.exp(s_w - m)
        p_c = jnp.exp(s_c - m)
        l = jnp.sum(p_w, axis=-1, keepdims=True) + jnp.sum(p_c, axis=-1, keepdims=True)
        o = _dot(p_w.astype(BF16), vw[:, sl]) + _dot(p_c.astype(BF16), vc[:, sl])
        outs.append(o / l)
    o_ref[...] = jnp.concatenate(outs, axis=1).astype(BF16)


def _na_call(qa, ka, va, kac, vac, bias_tab):
    b, n, aw = qa.shape
    rows = n // GRID_W
    kr = NA_WIN_ROWS
    assert rows >= kr
    lctx = kac.shape[1]
    win = kr * GRID_W

    def rstart(r):
        return jnp.clip(r - kr // 2, 0, rows - kr)

    win_spec = pl.BlockSpec((pl.Element(1), pl.Element(win), pl.Element(aw)),
                            lambda bb, r: (bb, rstart(r) * GRID_W, 0))
    ctx_spec = pl.BlockSpec((None, lctx, aw), lambda bb, r: (bb, 0, 0))
    return pl.pallas_call(
        _na_kernel,
        grid=(b, rows),
        in_specs=[pl.BlockSpec((None, GRID_W, aw), lambda bb, r: (bb, r, 0)),
                  win_spec, win_spec, ctx_spec, ctx_spec,
                  pl.BlockSpec((None, NA_HEADS, GRID_W, win), lambda bb, r: (r - rstart(r), 0, 0, 0))],
        out_specs=pl.BlockSpec((None, GRID_W, aw), lambda bb, r: (bb, r, 0)),
        out_shape=jax.ShapeDtypeStruct((b, n, aw), BF16),
        compiler_params=_cparams(("parallel", "arbitrary")),
        name="na_attn",
    )(qa, ka, va, kac, vac, bias_tab)


def _na_bias_table(rpb):
    kr, kc = NA_WIN_ROWS, NA_WIN_COLS
    c_idx = np.arange(GRID_W)
    c_start = np.clip(c_idx - kc // 2, 0, GRID_W - kc)
    kcol = np.arange(GRID_W)
    inside = (kcol[None, :] >= c_start[:, None]) & (kcol[None, :] < c_start[:, None] + kc)
    col_off = np.clip(kcol[None, :] - c_idx[:, None] + (NA_WIN_COLS - 1), 0, 2 * NA_WIN_COLS - 2)
    cls = np.arange(kr)[:, None]
    j = np.arange(kr)[None, :]
    row_off = j - cls + (NA_WIN_ROWS - 1)
    t = rpb[:, row_off][:, :, :, col_off]
    t = jnp.where(jnp.asarray(inside)[None, None, None], t, -1e30)
    t = jnp.transpose(t, (1, 0, 3, 2, 4))
    return t.reshape(kr, rpb.shape[0], GRID_W, kr * GRID_W).astype(F32)


def _flash_kernel(*refs, tq, tk, nk_main, diff, lam_init):
    it = iter(refs)
    q_ref = next(it)
    kc_ref = next(it)
    vc_ref = next(it)
    k_ref = v_ref = None
    if nk_main:
        k_ref = next(it)
        v_ref = next(it)
    if diff:
        lq1_ref, lk1_ref, lq2_ref, lk2_ref, gsub_ref = (next(it) for _ in range(5))
    o_ref = next(it)
    state = list(it)
    ngroups = len(state) // 3

    qt = q_ref[...]
    if diff:
        row = lax.broadcasted_iota(I32, qt.shape, 0)
        zero = jnp.zeros_like(qt)
        qt = jnp.concatenate([jnp.where(row < DIFF_D, qt, zero), jnp.where(row >= DIFF_D, qt, zero)], axis=1)
    gw = qt.shape[1] // ngroups
    for g in range(ngroups):
        m_ref, l_ref, acc_ref = state[3 * g:3 * g + 3]
        m_ref[...] = jnp.full(m_ref.shape, -1e30, F32)
        l_ref[...] = jnp.zeros(l_ref.shape, F32)
        acc_ref[...] = jnp.zeros(acc_ref.shape, F32)

    def block(kb, vts):
        s, pb, alpha = {}, {}, {}
        for t in range(ngroups + 2):
            if t < ngroups:
                s[t] = _dot(kb, qt[:, t * gw:(t + 1) * gw])
            g = t - 1
            if 0 <= g < ngroups:
                m_ref, l_ref, _ = state[3 * g:3 * g + 3]
                m_old = m_ref[...]
                m_new = jnp.maximum(m_old, jnp.max(s[g], axis=0, keepdims=True))
                alpha[g] = jnp.exp2(m_old - m_new)
                p = jnp.exp2(s.pop(g) - m_new)
                l_ref[...] = alpha[g] * l_ref[...] + jnp.sum(p, axis=0, keepdims=True)
                m_ref[...] = m_new
                pb[g] = p.astype(BF16)
            g = t - 2
            if 0 <= g < ngroups:
                acc_ref = state[3 * g + 2]
                pg = pb.pop(g)
                ck = vts[0].shape[1]
                pv = _dot(vts[0], pg[:ck])
                for c in range(1, len(vts)):
                    pv = pv + _dot(vts[c], pg[c * ck:(c + 1) * ck])
                acc_ref[...] = alpha.pop(g) * acc_ref[...] + pv

    block(kc_ref[...], [vc_ref[0]])
    if nk_main:
        nch = tk // v_ref.shape[2]

        def body(j, carry):
            off = pl.multiple_of(j * tk, tk)
            block(k_ref[pl.ds(off, tk), :], [v_ref[j * nch + c] for c in range(nch)])
            return carry
        lax.fori_loop(0, nk_main // tk, body, 0)

    o = jnp.concatenate([state[3 * g + 2][...] / state[3 * g + 1][...] for g in range(ngroups)], axis=1)
    if diff:
        lam = (jnp.exp(jnp.sum(lq1_ref[...] * lk1_ref[...], axis=-1, keepdims=True))
               - jnp.exp(jnp.sum(lq2_ref[...] * lk2_ref[...], axis=-1, keepdims=True)) + lam_init)
        o = o[:, :tq] - lam * o[:, tq:]
        o = o * lax.rsqrt(jnp.mean(o * o, axis=0, keepdims=True) + EPS) * gsub_ref[...] * (1.0 - lam_init)
    o_ref[...] = o.T.astype(BF16)


def _pick_tile(n, cands):
    for c in cands:
        if n % c == 0:
            return c
    return n


def _flash_call(qt, kc, vtc, k, vt, *, tq, diff=False, lam_params=None, lam_init=0.0, name="flash"):
    b, nh, _, n = qt.shape
    hw = nh * HEAD_PAD
    lctx = kc.shape[1]
    nk_main = 0 if k is None else k.shape[1]
    tq = min(tq, n)
    tk = 0
    if nk_main:
        chunk = vt.shape[-1]
        tk = chunk * _pick_tile(nk_main // chunk, (2, 1))
    m_rows = 2 * tq if diff else tq
    gw = min(QUERY_GROUP, m_rows)
    qspec = pl.BlockSpec((None, None, HEAD_PAD, tq), lambda bb, hh, i: (bb, hh, 0, i))
    ospec = pl.BlockSpec((None, tq, HEAD_PAD), lambda bb, hh, i: (bb, i, hh))
    in_specs = [qspec,
                pl.BlockSpec((None, lctx, HEAD_PAD), lambda bb, hh, i: (bb, 0, hh)),
                pl.BlockSpec((None, None, 1, HEAD_PAD, lctx), lambda bb, hh, i: (bb, hh, 0, 0, 0))]
    args = [qt, kc, vtc]
    if nk_main:
        in_specs += [pl.BlockSpec((None, nk_main, HEAD_PAD), lambda bb, hh, i: (bb, 0, hh)),
                     pl.BlockSpec((None, None) + vt.shape[2:], lambda bb, hh, i: (bb, hh, 0, 0, 0))]
        args += [k, vt]
    if diff:
        small = lambda a: pl.BlockSpec(a.shape, lambda bb, hh, i: (0,) * a.ndim)
        in_specs += [small(a) for a in lam_params]
        args += list(lam_params)
    kern = functools.partial(_flash_kernel, tq=tq, tk=tk, nk_main=nk_main, diff=diff, lam_init=lam_init)
    return pl.pallas_call(
        kern,
        grid=(b, nh, n // tq),
        in_specs=in_specs,
        out_specs=ospec,
        out_shape=jax.ShapeDtypeStruct((b, n, hw), BF16),
        scratch_shapes=[pltpu.VMEM(shape, F32) for _ in range(m_rows // gw)
                        for shape in ((1, gw), (1, gw), (HEAD_PAD, gw))],
        compiler_params=_cparams(("parallel", "parallel", "arbitrary")),
        name=name,
    )(*args)


def _merge_kernel(x_ref, sh1_ref, sc1_ref, g1_ref, sh2_ref, sc2_ref, ya_ref, yb_ref, yc_ref,
                  wg_ref, bg_ref, wa_ref, wb_ref, wc_ref, wo_ref, lng_ref, lnb_ref, wr_ref,
                  x1_ref, h2_ref, aff_ref, *, alpha):
    x = x_ref[...]
    d = x.shape[-1]
    hb = (_ln_core(x) * (1.0 + sc1_ref[...]) + sh1_ref[...]).astype(BF16)

    def gate(i):
        return jax.nn.sigmoid(_dot(hb, wg_ref[:, i * d:(i + 1) * d]) + bg_ref[:, i * d:(i + 1) * d])

    m = gate(0) * _dot(ya_ref[...], wa_ref[...])
    m = m + gate(1) * _dot(yb_ref[...], wb_ref[...])
    m = m + gate(2) * _dot(yc_ref[...], wc_ref[...])
    y = _dot(m.astype(BF16), wo_ref[...])
    x1 = _ln_core(alpha * x + g1_ref[...] * y) * lng_ref[...] + lnb_ref[...]
    x1_ref[...] = x1
    h2 = _ln_core(x1) * (1.0 + sc2_ref[...]) + sh2_ref[...]
    h2_ref[...] = h2.astype(BF16)
    logits = jnp.dot(h2, wr_ref[...], precision=lax.Precision.HIGHEST, preferred_element_type=F32)
    lane = lax.broadcasted_iota(I32, logits.shape, 1)
    logits = jnp.where(lane < N_EXPERTS, logits, -1e30)
    e = jnp.exp(logits - jnp.max(logits, axis=-1, keepdims=True))
    aff = e / jnp.sum(e, axis=-1, keepdims=True)
    aff_ref[...] = aff[:, :N_EXPERTS]


def _merge_call(x, mods, ya, yb, yc, wp, alpha, tm):
    b, n, d = x.shape
    tm = min(tm, n)
    row = lambda w: pl.BlockSpec((None, tm, w), lambda bb, i: (bb, i, 0))
    vec = pl.BlockSpec((None, 1, d), lambda bb, i: (bb, 0, 0))
    full = lambda a: pl.BlockSpec(a.shape, lambda bb, i: (0,) * a.ndim)
    ws = [wp["w_gate"], wp["b_gate"], wp["w_br_a"], wp["w_br_b"], wp["w_br_c"], wp["w_out"],
          wp["ln1_g"], wp["ln1_b"], wp["w_router"]]
    return pl.pallas_call(
        functools.partial(_merge_kernel, alpha=alpha),
        grid=(b, n // tm),
        in_specs=[row(d)] + [vec] * 5 + [row(ya.shape[-1]), row(yb.shape[-1]), row(yc.shape[-1])]
        + [full(a) for a in ws],
        out_specs=[row(d), row(d), row(N_EXPERTS)],
        out_shape=[jax.ShapeDtypeStruct((b, n, d), F32), jax.ShapeDtypeStruct((b, n, d), BF16),
                   jax.ShapeDtypeStruct((b, n, N_EXPERTS), F32)],
        compiler_params=_cparams(("parallel", "parallel")),
        name="merge",
    )(x, *mods, ya, yb, yc, *ws)


def _topk_kernel(a_ref, posm_ref, pos_ref, *, cap):
    nck = a_ref.shape[0]

    def count(pred):
        bits = lax.bitcast_convert_type(a_ref[...], I32)
        c = jnp.sum(pred(bits).astype(F32), axis=0)
        return jnp.sum(c, axis=-1, keepdims=True)

    def bit_step(i, t):
        cand = t | jnp.left_shift(jnp.int32(1), 30 - i)
        return jnp.where(count(lambda bits: bits >= cand[None]) >= cap, cand, t)

    t = lax.fori_loop(0, 31, bit_step, jnp.zeros((N_EXPERTS, 1), I32))
    need = cap - count(lambda bits: bits > t[None])

    r = lax.broadcasted_iota(I32, (LANE, LANE), 0)
    c = lax.broadcasted_iota(I32, (LANE, LANE), 1)
    u_incl = (r <= c).astype(BF16)
    u_excl = (r < c).astype(BF16)

    def chunk_step(ci, carry):
        run_eq, run_sel = carry
        bc = lax.bitcast_convert_type(a_ref[ci], I32)
        gt = bc > t
        eq = bc == t
        eq_before = _dot(eq.astype(BF16), u_excl) + run_eq
        sel = gt | (eq & (eq_before < need))
        sel_incl = (_dot(sel.astype(BF16), u_incl) + run_sel).astype(I32)
        pos_ref[ci] = sel_incl
        posm_ref[ci] = jnp.where(sel, sel_incl - 1, -1)
        run_eq = run_eq + jnp.sum(eq.astype(F32), axis=-1, keepdims=True)
        run_sel = run_sel + jnp.sum(sel.astype(F32), axis=-1, keepdims=True)
        return run_eq, run_sel

    z = jnp.zeros((N_EXPERTS, 1), F32)
    lax.fori_loop(0, nck, chunk_step, (z, z))


def _topk_call(aff_ck, cap):
    b, nck, e, _ = aff_ck.shape
    spec = pl.BlockSpec((None, nck, e, LANE), lambda bb: (bb, 0, 0, 0))
    return pl.pallas_call(
        functools.partial(_topk_kernel, cap=cap),
        grid=(b,),
        in_specs=[spec],
        out_specs=[spec, spec],
        out_shape=[jax.ShapeDtypeStruct(aff_ck.shape, I32)] * 2,
        compiler_params=_cparams(("parallel",)),
        name="topk",
    )(aff_ck)


def _gather_kernel(chunk_ref, tile_ref, valid_ref, first_ref, posm_ref, h_ref, o_ref, acc_ref, *, cj, steps):
    bb, e, s = pl.program_id(0), pl.program_id(1), pl.program_id(2)
    idx = (bb * pl.num_programs(1) + e) * steps + s

    @pl.when(first_ref[idx] == 1)
    def _():
        acc_ref[...] = jnp.zeros_like(acc_ref)

    @pl.when(valid_ref[idx] == 1)
    def _():
        tt = h_ref.shape[0]
        slot = chunk_ref[idx] * cj + lax.broadcasted_iota(I32, (cj, tt), 0)
        onehot = (posm_ref[...] == slot).astype(BF16)
        acc_ref[...] += _dot(onehot, h_ref[...])

    o_ref[...] = acc_ref[...].astype(BF16)


def _gather_call(wl, posm_rows, h2, cap, cj, tt):
    b, n, d = h2.shape
    e = N_EXPERTS
    steps = wl["g_chunk"].shape[0] // (b * e)
    grid_spec = pltpu.PrefetchScalarGridSpec(
        num_scalar_prefetch=4,
        grid=(b, e, steps),
        in_specs=[
            pl.BlockSpec((None, None, None, 1, tt),
                         lambda bb, ee, s, ch, tl, va, fi: (bb, ee, tl[(bb * e + ee) * steps + s], 0, 0)),
            pl.BlockSpec((None, tt, d), lambda bb, ee, s, ch, tl, va, fi: (bb, tl[(bb * e + ee) * steps + s], 0)),
        ],
        out_specs=pl.BlockSpec((None, None, cj, d),
                               lambda bb, ee, s, ch, tl, va, fi: (bb, ee, ch[(bb * e + ee) * steps + s], 0)),
        scratch_shapes=[pltpu.VMEM((cj, d), F32)],
    )
    return pl.pallas_call(
        functools.partial(_gather_kernel, cj=cj, steps=steps),
        grid_spec=grid_spec,
        out_shape=jax.ShapeDtypeStruct((b, e, cap, d), BF16),
        compiler_params=_cparams(("arbitrary", "arbitrary", "arbitrary")),
        name="moe_gather",
    )(wl["g_chunk"], wl["g_tile"], wl["g_valid"], wl["g_first"], posm_rows, h2)


def _ffn_kernel(x_ref, wg_ref, wu_ref, wd_ref, o_ref, acc_ref):
    f = pl.program_id(2)

    @pl.when(f == 0)
    def _():
        acc_ref[...] = jnp.zeros_like(acc_ref)

    x = x_ref[...]
    a = _dot(x, wg_ref[...].astype(BF16))
    u = _dot(x, wu_ref[...].astype(BF16))
    hmid = (a * jax.nn.sigmoid(a) * u).astype(BF16)
    acc_ref[...] += _dot(hmid, wd_ref[...].astype(BF16))

    @pl.when(f == pl.num_programs(2) - 1)
    def _():
        o_ref[...] = acc_ref[...].astype(BF16)


def _ffn_call(xsel, w_g, w_u, w_d):
    b, e, cap, d = xsel.shape
    ff = w_g.shape[-1]
    tf = _pick_tile(ff, (256, 128))
    xspec = pl.BlockSpec((None, None, cap, d), lambda ee, bb, f: (bb, ee, 0, 0))
    return pl.pallas_call(
        _ffn_kernel,
        grid=(e, b, ff // tf),
        in_specs=[xspec,
                  pl.BlockSpec((None, d, tf), lambda ee, bb, f: (ee, 0, f)),
                  pl.BlockSpec((None, d, tf), lambda ee, bb, f: (ee, 0, f)),
                  pl.BlockSpec((None, tf, d), lambda ee, bb, f: (ee, f, 0))],
        out_specs=xspec,
        out_shape=jax.ShapeDtypeStruct(xsel.shape, BF16),
        scratch_shapes=[pltpu.VMEM((cap, d), F32)],
        compiler_params=_cparams(("parallel", "parallel", "arbitrary")),
        name="moe_ffn",
    )(xsel, w_g, w_u, w_d)


def _combine_kernel(tile_ref, exp_ref, chunk_ref, valid_ref, first_ref, last_ref,
                    posm_ref, aff_ref, y_ref, x_ref, g2_ref, lng_ref, lnb_ref, o_ref, acc_ref,
                    *, cj, steps, alpha):
    idx = pl.program_id(0) * steps + pl.program_id(1)

    @pl.when(first_ref[idx] == 1)
    def _():
        acc_ref[...] = jnp.zeros_like(acc_ref)

    @pl.when(valid_ref[idx] == 1)
    def _():
        tt = acc_ref.shape[0]
        lane = lax.broadcasted_iota(I32, (tt, N_EXPERTS), 1)
        pick = lane == exp_ref[idx]
        slot = jnp.sum(jnp.where(pick, posm_ref[...].astype(F32), 0.0), axis=-1, keepdims=True)
        gate = jnp.sum(jnp.where(pick, aff_ref[...], 0.0), axis=-1, keepdims=True)
        cols = chunk_ref[idx] * cj + lax.broadcasted_iota(I32, (tt, cj), 1)
        onehot = (slot == cols.astype(F32)).astype(BF16)
        acc_ref[...] += gate * _dot(onehot, y_ref[...])

    @pl.when(last_ref[idx] == 1)
    def _():
        o_ref[...] = _ln_core(alpha * x_ref[...] + g2_ref[...] * acc_ref[...]) * lng_ref[...] + lnb_ref[...]


def _combine_call(wl, posm_tm, aff_tm, y, x1, g2, ln_g, ln_b, alpha, cj, tt):
    b, n, d = x1.shape
    e = N_EXPERTS
    steps = wl["c_tile"].shape[0] // b
    tok = lambda w: pl.BlockSpec((None, tt, w), lambda bb, s, tl, ex, ch, va, fi, la: (bb, tl[bb * steps + s], 0))
    small = lambda a: pl.BlockSpec(a.shape, lambda bb, s, tl, ex, ch, va, fi, la: (0,) * a.ndim)
    grid_spec = pltpu.PrefetchScalarGridSpec(
        num_scalar_prefetch=6,
        grid=(b, steps),
        in_specs=[tok(e), tok(e),
                  pl.BlockSpec((None, None, cj, d),
                               lambda bb, s, tl, ex, ch, va, fi, la: (bb, ex[bb * steps + s], ch[bb * steps + s], 0)),
                  tok(d),
                  pl.BlockSpec((None, 1, d), lambda bb, s, tl, ex, ch, va, fi, la: (bb, 0, 0)),
                  small(ln_g), small(ln_b)],
        out_specs=tok(d),
        scratch_shapes=[pltpu.VMEM((tt, d), F32)],
    )
    return pl.pallas_call(
        functools.partial(_combine_kernel, cj=cj, steps=steps, alpha=alpha),
        grid_spec=grid_spec,
        out_shape=jax.ShapeDtypeStruct((b, n, d), F32),
        compiler_params=_cparams(("arbitrary", "arbitrary")),
        name="moe_combine",
    )(wl["c_tile"], wl["c_exp"], wl["c_chunk"], wl["c_valid"], wl["c_first"], wl["c_last"],
      posm_tm, aff_tm, y, x1, g2, ln_g, ln_b)


def _work_lists(pos_rows, cap, cj, tt):
    b, e, n = pos_rows.shape
    nt, nc = n // tt, cap // cj
    pend = pos_rows[:, :, tt - 1::tt]
    pstart = jnp.concatenate([jnp.zeros((b, e, 1), I32), pend[:, :, :-1]], axis=-1)
    nonempty = pend > pstart
    clo = jnp.minimum(pstart // cj, nc - 1)
    chi = jnp.maximum(pend - 1, 0) // cj
    npairs = jnp.where(nonempty, chi - clo + 1, 0)

    sg = nt + nc
    cum = jnp.cumsum(npairs, axis=-1)
    total = cum[..., -1:]
    s = jnp.arange(sg, dtype=I32)[None, None, :]
    s_eff = jnp.minimum(s, total - 1)
    tile = jnp.sum((cum[..., None, :] <= s_eff[..., :, None]).astype(I32), axis=-1)
    off = jnp.take_along_axis(cum - npairs, tile, axis=-1)
    chunk = jnp.take_along_axis(clo, tile, axis=-1) + (s_eff - off)
    valid = (s < total).astype(I32)
    prev = jnp.concatenate([jnp.full((b, e, 1), -1, I32), chunk[..., :-1]], axis=-1)
    first = (chunk != prev).astype(I32)

    sc = e * (nt + nc)
    nent = jnp.maximum(jnp.transpose(npairs, (0, 2, 1)).reshape(b, nt * e), 1)
    npf = jnp.transpose(npairs, (0, 2, 1)).reshape(b, nt * e)
    clof = jnp.transpose(clo, (0, 2, 1)).reshape(b, nt * e)
    ccum = jnp.cumsum(nent, axis=-1)
    ctotal = ccum[:, -1:]
    s2 = jnp.arange(sc, dtype=I32)[None, :]
    s2_eff = jnp.minimum(s2, ctotal - 1)
    ent = jnp.sum((ccum[:, None, :] <= s2_eff[:, :, None]).astype(I32), axis=-1)
    coff = jnp.take_along_axis(ccum - nent, ent, axis=-1)
    c_tile = ent // e
    c_exp = ent % e
    c_chunk = jnp.minimum(jnp.take_along_axis(clof, ent, axis=-1) + (s2_eff - coff), nc - 1)
    live = s2 < ctotal
    c_valid = (live & (jnp.take_along_axis(npf, ent, axis=-1) > 0)).astype(I32)
    tprev = jnp.concatenate([jnp.full((b, 1), -1, I32), c_tile[:, :-1]], axis=-1)
    tnext = jnp.concatenate([c_tile[:, 1:], jnp.full((b, 1), -1, I32)], axis=-1)
    c_first = (live & (c_tile != tprev)).astype(I32)
    c_last = (live & ((c_tile != tnext) | (s2 == ctotal - 1))).astype(I32)
    flat = lambda a: a.reshape(-1).astype(I32)
    return dict(g_chunk=flat(chunk), g_tile=flat(tile), g_valid=flat(valid), g_first=flat(first),
                c_tile=flat(c_tile), c_exp=flat(c_exp), c_chunk=flat(c_chunk), c_valid=flat(c_valid),
                c_first=flat(c_first), c_last=flat(c_last))


def _moe(h2, aff_tm, x1, g2, ln_g, ln_b, w_g, w_u, w_d, alpha):
    b, n, d = h2.shape
    e = N_EXPERTS
    cap = max(1, EC_CAPACITY_FACTOR * n // e)
    cj = min(SLOT_CHUNK, cap)
    tt = min(TOKEN_TILE, n)
    nck = n // LANE
    aff_ck = jnp.transpose(aff_tm.reshape(b, nck, LANE, e), (0, 1, 3, 2))
    posm_ck, pos_ck = _topk_call(aff_ck, cap)
    posm_rows = jnp.transpose(posm_ck, (0, 2, 1, 3)).reshape(b, e, n)
    pos_rows = jnp.transpose(pos_ck, (0, 2, 1, 3)).reshape(b, e, n)
    posm_tm = jnp.transpose(posm_rows, (0, 2, 1))
    wl = _work_lists(pos_rows, cap, cj, tt)
    xsel = _gather_call(wl, posm_rows.reshape(b, e, n // tt, 1, tt), h2, cap, cj, tt)
    y = _ffn_call(xsel, w_g, w_u, w_d)
    return _combine_call(wl, posm_tm, aff_tm, y, x1, g2, ln_g, ln_b, alpha, cj, tt)


def _swap_perm(rot_dim):
    q = rot_dim // 4
    return np.concatenate([np.arange(q, 2 * q), np.arange(0, q), np.arange(3 * q, 4 * q), np.arange(2 * q, 3 * q)])


def _rope_tables(n, rot_dim):
    t = jnp.arange(n, dtype=I32)
    row = (t // GRID_W).astype(F32)
    col = (t % GRID_W).astype(F32)
    m = rot_dim // 2
    inv = ROPE_BASE ** (-jnp.arange(0, m, 2, dtype=F32) / m)
    ar, ac = row[:, None] * inv, col[:, None] * inv
    cos = jnp.concatenate([jnp.cos(ar), jnp.cos(ar), jnp.cos(ac), jnp.cos(ac)], axis=-1)
    sin = jnp.concatenate([-jnp.sin(ar), jnp.sin(ar), -jnp.sin(ac), jnp.sin(ac)], axis=-1)
    return cos, sin


def _head_tables(n, rope):
    if rope:
        cb, sb = _rope_tables(n, MLA_ROPE)
        cdd, sdd = _rope_tables(n, DIFF_D)
    else:
        cb, sb = jnp.ones((n, MLA_ROPE), F32), jnp.zeros((n, MLA_ROPE), F32)
        cdd, sdd = jnp.ones((n, DIFF_D), F32), jnp.zeros((n, DIFF_D), F32)
    pad = HEAD_PAD - MLA_NOPE - MLA_ROPE
    cm = jnp.concatenate([jnp.ones((n, MLA_NOPE), F32), cb, jnp.zeros((n, pad), F32)], axis=-1)
    sm = jnp.concatenate([jnp.zeros((n, MLA_NOPE), F32), sb, jnp.zeros((n, pad), F32)], axis=-1)
    return cm, sm, jnp.concatenate([cdd, cdd], axis=-1), jnp.concatenate([sdd, sdd], axis=-1)


def _prep_layer(w_in, g_q, g_kv, w_uq, w_ukv, w_br_b, w_router):
    d = w_in.shape[0]
    offs = np.cumsum([0, A_W, A_W, A_W, MLA_Q_RANK, MLA_KV_RANK, MLA_ROPE, C_QK, C_QK, C_W])
    seg = lambda i: w_in[:, offs[i]:offs[i + 1]]
    z = lambda w: jnp.zeros((d, w), w_in.dtype)
    sw32 = _swap_perm(MLA_ROPE)
    sw64 = _swap_perm(DIFF_D)
    chunk_sw = np.concatenate([c * DIFF_D + sw64 for c in range(C_QK // DIFF_D)])
    kr = seg(5)
    pad = HEAD_PAD - MLA_NOPE - MLA_ROPE
    w_ext = jnp.concatenate([
        seg(0), seg(1), seg(2), seg(3), seg(4),
        z(MLA_NOPE), kr, z(pad), z(MLA_NOPE), kr[:, sw32], z(pad),
        seg(6), seg(6)[:, chunk_sw], seg(7), seg(7)[:, chunk_sw], seg(8)], axis=1).astype(BF16)
    assert w_ext.shape[1] == W_EXT

    qd = MLA_NOPE + MLA_ROPE
    wq = w_uq.reshape(MLA_Q_RANK, MLA_HEADS, qd)
    zq = lambda w: jnp.zeros((MLA_Q_RANK, MLA_HEADS, w), w_uq.dtype)
    wuq1 = jnp.concatenate([wq, zq(pad)], axis=-1).reshape(MLA_Q_RANK, MLA_HEADS * HEAD_PAD).astype(BF16)
    wuq2 = jnp.concatenate([zq(MLA_NOPE), wq[:, :, MLA_NOPE:][:, :, sw32], zq(pad)],
                           axis=-1).reshape(MLA_Q_RANK, MLA_HEADS * HEAD_PAD).astype(BF16)
    wkv = w_ukv.reshape(MLA_KV_RANK, MLA_HEADS, MLA_NOPE + MLA_V)
    zk = jnp.zeros((MLA_KV_RANK, MLA_HEADS, HEAD_PAD - MLA_NOPE), w_ukv.dtype)
    zv = jnp.zeros((MLA_KV_RANK, MLA_HEADS, HEAD_PAD - MLA_V), w_ukv.dtype)
    wkvk = jnp.concatenate([wkv[:, :, :MLA_NOPE], zk], axis=-1).reshape(MLA_KV_RANK, -1).astype(BF16)
    wkvv = jnp.concatenate([wkv[:, :, MLA_NOPE:], zv], axis=-1).reshape(MLA_KV_RANK, -1).astype(BF16)
    wb = w_br_b.reshape(MLA_HEADS, MLA_V, d)
    wb = jnp.concatenate([wb, jnp.zeros((MLA_HEADS, HEAD_PAD - MLA_V, d), wb.dtype)], axis=1)
    wb = wb.reshape(MLA_HEADS * HEAD_PAD, d).astype(BF16)
    wr = jnp.concatenate([w_router, jnp.zeros((d, LANE - N_EXPERTS), w_router.dtype)], axis=1)
    return dict(w_in=w_ext, g_q=g_q.reshape(1, -1), g_kv=g_kv.reshape(1, -1), wuq1=wuq1, wuq2=wuq2,
                wkvk=wkvk, wkvv=wkvv, w_br_b=wb, w_router=wr)


def _pad_heads(t, nh):
    b, n, w = t.shape
    hd = w // nh
    t = t.reshape(b, n, nh, hd)
    t = jnp.concatenate([t, jnp.zeros((b, n, nh, HEAD_PAD - hd), t.dtype)], axis=-1)
    return t.reshape(b, n, nh * HEAD_PAD)


def _unpad_heads(t, nh, hd):
    b, n, _ = t.shape
    return t.reshape(b, n, nh, HEAD_PAD)[..., :hd].reshape(b, n, nh * hd)


def kernel(x, c, ctx, c_ctx, w_ada, b_ada, w_in, na_rpb, mla_g_q, mla_g_kv, mla_w_uq, mla_w_ukv,
           diff_lq1, diff_lk1, diff_lq2, diff_lk2, diff_g_sub, w_br_a, w_br_b, w_br_c,
           w_gate, b_gate, w_out, ln1_g, ln1_b, w_router, w_exp_gate, w_exp_up, w_exp_down,
           ln2_g, ln2_b):
    depth = w_ada.shape[0]
    b, n, d = x.shape
    lctx = ctx.shape[1]
    alpha = (2 * depth) ** 0.25

    cvec = jnp.concatenate([c, jnp.broadcast_to(c_ctx[None], (8 - b, d))], axis=0)
    mod_all = _ada_call(cvec, w_ada, b_ada)
    tabs_x = _head_tables(n, True)
    tabs_c = _head_tables(lctx, False)

    for l in range(depth):
        last = l == depth - 1
        lam_init = 0.8 - 0.6 * math.exp(-0.3 * l)
        mod = mod_all[l, :b].reshape(b, 1, 6 * d)
        mod_c = jnp.broadcast_to(mod_all[l, b:b + 1].reshape(1, 1, 6 * d), (b, 1, 6 * d))
        mods = [mod[..., i * d:(i + 1) * d] for i in range(6)]
        mods_c = [mod_c[..., i * d:(i + 1) * d] for i in range(6)]
        wp = _prep_layer(w_in[l], mla_g_q[l], mla_g_kv[l], mla_w_uq[l], mla_w_ukv[l], w_br_b[l], w_router[l])
        wp.update(w_gate=w_gate[l].astype(BF16), b_gate=b_gate[l].reshape(1, -1),
                  w_br_a=w_br_a[l].astype(BF16), w_br_c=w_br_c[l].astype(BF16), w_out=w_out[l].astype(BF16),
                  ln1_g=ln1_g[l].reshape(1, -1), ln1_b=ln1_b[l].reshape(1, -1))
        lam_params = [diff_lq1[l].reshape(1, -1), diff_lk1[l].reshape(1, -1), diff_lq2[l].reshape(1, -1),
                      diff_lk2[l].reshape(1, -1), diff_g_sub[l].reshape(-1, 1)]
        ln2g, ln2b = ln2_g[l].reshape(1, -1), ln2_b[l].reshape(1, -1)

        zx = _inproj_call(x, mods[0], mods[1], wp, tabs_x, min(INPROJ_TILE, n))
        zc = _inproj_call(ctx, mods_c[0], mods_c[1], wp, tabs_c, lctx)
        qa, ka, va, qm, km, vm, dq, dk, dv = zx
        qa_c, ka_c, va_c, qm_c, km_c, vm_c, dq_c, dk_c, dv_c = zc

        ya = _na_call(qa, ka, va, ka_c, va_c, _na_bias_table(na_rpb[l]))
        yb = _flash_call(qm, km_c, vm_c, km, vm, tq=4 * QUERY_GROUP, name="mla_attn")
        yc = _flash_call(dq, dk_c, dv_c, dk, dv, tq=2 * QUERY_GROUP, diff=True, lam_params=lam_params,
                         lam_init=lam_init, name="diff_attn")
        x1, h2, aff = _merge_call(x, mods[:5], ya, yb, yc, wp, alpha, 512)
        x = _moe(h2, aff, x1, mods[5], ln2g, ln2b, w_exp_gate[l], w_exp_up[l], w_exp_down[l], alpha)

        if not last:
            heads_t = lambda t: jnp.transpose(_pad_heads(t, NA_HEADS).reshape(b, lctx, NA_HEADS, HEAD_PAD),
                                              (0, 2, 3, 1))
            qa_c2 = (qa_c.astype(F32) * LOG2E).astype(BF16)
            ya_c = _unpad_heads(
                _flash_call(heads_t(qa_c2), _pad_heads(ka_c, NA_HEADS), heads_t(va_c)[:, :, None],
                            None, None, tq=lctx, name="ctx_na_attn"), NA_HEADS, HEAD_DIM)
            yb_c = _flash_call(qm_c, km_c, vm_c, None, None, tq=lctx, name="ctx_mla_attn")
            yc_c = _flash_call(dq_c, dk_c, dv_c, None, None, tq=lctx, diff=True, lam_params=lam_params,
                               lam_init=lam_init, name="ctx_diff_attn")
            c1, hc2, aff_c = _merge_call(ctx, mods_c[:5], ya_c, yb_c, yc_c, wp, alpha, lctx)
            ctx = _moe(hc2, aff_c, c1, mods_c[5], ln2g, ln2b, w_exp_gate[l], w_exp_up[l], w_exp_down[l], alpha)
    return x
```

```python
import functools
import math

import jax
import jax.numpy as jnp
import numpy as np
from jax import lax
from jax.experimental import pallas as pl
from jax.experimental.pallas import tpu as pltpu

F32 = jnp.float32
BF16 = jnp.bfloat16
I32 = jnp.int32

GRID_W = 64
HEAD_DIM = 64
NA_HEADS = 8
NA_WIN_ROWS = 8
NA_WIN_COLS = 16
MLA_HEADS = 8
MLA_Q_RANK = 256
MLA_KV_RANK = 128
MLA_NOPE = 64
MLA_ROPE = 32
MLA_V = 64
DIFF_HEADS = 4
DIFF_D = 64
DIFF_V = 2 * DIFF_D
A_W = NA_HEADS * HEAD_DIM
C_QK = DIFF_HEADS * 2 * DIFF_D
C_W = DIFF_HEADS * DIFF_V
N_BRANCH = 3
N_EXPERTS = 16
EC_CAPACITY_FACTOR = 2
ROPE_BASE = 10000.0
EPS = 1e-6
LOG2E = math.log2(math.e)

LANE = 128
HEAD_PAD = 128
VMEM_LIMIT = 56 * 1024 * 1024
SLOT_CHUNK = 256
TOKEN_TILE = 512
QUERY_GROUP = 256
INPROJ_TILE = 512

OFF_QA, OFF_KA, OFF_VA = 0, 512, 1024
OFF_CQ = 1536
OFF_CKV = 1792
OFF_KR, OFF_KRS = 1920, 2048
OFF_DQ, OFF_DQS = 2176, 2688
OFF_DK, OFF_DKS = 3200, 3712
OFF_DV = 4224
W_EXT = 4736


def _cparams(sem, vmem=VMEM_LIMIT):
    return pltpu.CompilerParams(dimension_semantics=sem, vmem_limit_bytes=vmem)


def _ln_core(x):
    mu = jnp.mean(x, axis=-1, keepdims=True)
    xc = x - mu
    var = jnp.mean(xc * xc, axis=-1, keepdims=True)
    return xc * lax.rsqrt(var + EPS)


def _dot(a, b):
    return jnp.dot(a, b, preferred_element_type=F32)


def _dot_nt(a, b):
    return lax.dot_general(a, b, (((1,), (1,)), ((), ())), preferred_element_type=F32)


def _ada_kernel(c_ref, w_ref, b_ref, o_ref):
    c = c_ref[...]
    s = c * jax.nn.sigmoid(c)
    o_ref[...] = jnp.dot(s, w_ref[...], precision=lax.Precision.HIGHEST,
                         preferred_element_type=F32) + b_ref[...]


def _ada_call(cvec, w_ada, b_ada):
    nl, d, d6 = w_ada.shape
    tn = _pick_tile(d6, (1024, 768, 512, 256, 128))
    return pl.pallas_call(
        _ada_kernel,
        grid=(nl, d6 // tn),
        in_specs=[pl.BlockSpec((8, d), lambda l, j: (0, 0)),
                  pl.BlockSpec((None, d, tn), lambda l, j: (l, 0, j)),
                  pl.BlockSpec((None, 1, tn), lambda l, j: (l, 0, j))],
        out_specs=pl.BlockSpec((None, 8, tn), lambda l, j: (l, 0, j)),
        out_shape=jax.ShapeDtypeStruct((nl, 8, d6), F32),
        compiler_params=_cparams(("parallel", "parallel")),
        name="ada_mod",
    )(cvec, w_ada, b_ada.reshape(nl, 1, d6))


def _inproj_kernel(x_ref, sh_ref, sc_ref, w_ref, gq_ref, gkv_ref, wuq1_ref, wuq2_ref,
                   wkvk_ref, wkvv_ref, cm_ref, sm_ref, cd_ref, sd_ref,
                   qa_ref, ka_ref, va_ref, qm_ref, km_ref, vm_ref, dq_ref, dk_ref, dv_ref,
                   *, scale_a, scale_b, scale_c):
    h = _ln_core(x_ref[...]) * (1.0 + sc_ref[...]) + sh_ref[...]
    hb = h.astype(BF16)

    def seg(off, width):
        return _dot(hb, w_ref[:, off:off + width])

    qa_ref[...] = (seg(OFF_QA, A_W) * scale_a).astype(BF16)
    ka_ref[...] = seg(OFF_KA, A_W).astype(BF16)
    va_ref[...] = seg(OFF_VA, A_W).astype(BF16)

    cm = cm_ref[...]
    sm = sm_ref[...]
    cd = cd_ref[...]
    sd = sd_ref[...]

    cq = seg(OFF_CQ, MLA_Q_RANK)
    cqn = (cq * lax.rsqrt(jnp.mean(cq * cq, axis=-1, keepdims=True) + EPS) * gq_ref[...]).astype(BF16)
    q1 = _dot(cqn, wuq1_ref[...])
    q2 = _dot(cqn, wuq2_ref[...])
    ckv = seg(OFF_CKV, MLA_KV_RANK)
    ckvn = (ckv * lax.rsqrt(jnp.mean(ckv * ckv, axis=-1, keepdims=True) + EPS) * gkv_ref[...]).astype(BF16)
    kn = _dot(ckvn, wkvk_ref[...])
    vm = _dot(ckvn, wkvv_ref[...])
    kr = seg(OFF_KR, HEAD_PAD) * cm + seg(OFF_KRS, HEAD_PAD) * sm
    for hh in range(MLA_HEADS):
        sl = slice(hh * HEAD_PAD, (hh + 1) * HEAD_PAD)
        qm_ref[hh] = ((q1[:, sl] * cm + q2[:, sl] * sm) * scale_b).T.astype(BF16)
        km_ref[:, sl] = (kn[:, sl] + kr).astype(BF16)
        vm_ref[hh] = vm[:, sl].T.astype(BF16)

    dq = seg(OFF_DQ, C_QK)
    dqs = seg(OFF_DQS, C_QK)
    dk = seg(OFF_DK, C_QK)
    dks = seg(OFF_DKS, C_QK)
    dv = seg(OFF_DV, C_W)
    for hh in range(DIFF_HEADS):
        sl = slice(hh * HEAD_PAD, (hh + 1) * HEAD_PAD)
        dq_ref[hh] = ((dq[:, sl] * cd + dqs[:, sl] * sd) * scale_c).T.astype(BF16)
        dk_ref[:, sl] = (dk[:, sl] * cd + dks[:, sl] * sd).astype(BF16)
        dv_ref[hh] = dv[:, sl].T.astype(BF16)


def _inproj_call(x, sh, sc, wp, tabs, tm):
    b, n, d = x.shape
    nt = n // tm
    cm, sm, cd, sd = tabs
    row = lambda w: pl.BlockSpec((None, tm, w), lambda i, bb: (bb, i, 0))
    vec = lambda w: pl.BlockSpec((None, 1, w), lambda i, bb: (bb, 0, 0))
    full = lambda a: pl.BlockSpec(a.shape, lambda i, bb: (0,) * a.ndim)
    tab = pl.BlockSpec((tm, HEAD_PAD), lambda i, bb: (i, 0))
    qt = lambda nh: (pl.BlockSpec((None, nh, HEAD_PAD, tm), lambda i, bb: (bb, 0, 0, i)),
                     jax.ShapeDtypeStruct((b, nh, HEAD_PAD, n), BF16))
    vt = lambda nh: (pl.BlockSpec((None, nh, None, HEAD_PAD, tm), lambda i, bb: (bb, 0, i, 0, 0)),
                     jax.ShapeDtypeStruct((b, nh, nt, HEAD_PAD, tm), BF16))
    tok = lambda w: (row(w), jax.ShapeDtypeStruct((b, n, w), BF16))
    outs = [tok(A_W), tok(A_W), tok(A_W), qt(MLA_HEADS), tok(MLA_HEADS * HEAD_PAD), vt(MLA_HEADS),
            qt(DIFF_HEADS), tok(C_QK), vt(DIFF_HEADS)]
    kern = functools.partial(_inproj_kernel, scale_a=HEAD_DIM ** -0.5,
                             scale_b=LOG2E * (MLA_NOPE + MLA_ROPE) ** -0.5, scale_c=LOG2E * DIFF_D ** -0.5)
    return pl.pallas_call(
        kern,
        grid=(nt, b),
        in_specs=[row(d), vec(d), vec(d), full(wp["w_in"]), full(wp["g_q"]), full(wp["g_kv"]),
                  full(wp["wuq1"]), full(wp["wuq2"]), full(wp["wkvk"]), full(wp["wkvv"]),
                  tab, tab, tab, tab],
        out_specs=[o[0] for o in outs],
        out_shape=[o[1] for o in outs],
        compiler_params=_cparams(("parallel", "parallel")),
        name="inproj",
    )(x, sh, sc, wp["w_in"], wp["g_q"], wp["g_kv"], wp["wuq1"], wp["wuq2"], wp["wkvk"], wp["wkvv"],
      cm, sm, cd, sd)


def _na_kernel(q_ref, kw_ref, vw_ref, kc_ref, vc_ref, bias_ref, o_ref):
    q = q_ref[...]
    kw = kw_ref[0]
    vw = vw_ref[0]
    kc = kc_ref[...]
    vc = vc_ref[...]
    outs = []
    for hh in range(NA_HEADS):
        sl = slice(hh * HEAD_DIM, (hh + 1) * HEAD_DIM)
        qh = q[:, sl]
        s_w = _dot_nt(qh, kw[:, sl]) + bias_ref[hh]
        s_c = _dot_nt(qh, kc[:, sl])
        m = jnp.maximum(jnp.max(s_w, axis=-1, keepdims=True), jnp.max(s_c, axis=-1, keepdims=True))
        p_w = jnp.exp(s_w - m)
        p_c = jnp.exp(s_c - m)
        l = jnp.sum(p_w, axis=-1, keepdims=True) + jnp.sum(p_c, axis=-1, keepdims=True)
        o = _dot(p_w.astype(BF16), vw[:, sl]) + _dot(p_c.astype(BF16), vc[:, sl])
        outs.append(o / l)
    o_ref[...] = jnp.concatenate(outs, axis=1).astype(BF16)


def _na_call(qa, ka, va, kac, vac, bias_tab):
    b, n, aw = qa.shape
    rows = n // GRID_W
    kr = NA_WIN_ROWS
    assert rows >= kr
    lctx = kac.shape[1]
    win = kr * GRID_W

    def rstart(r):
        return jnp.clip(r - kr // 2, 0, rows - kr)

    win_spec = pl.BlockSpec((pl.Element(1), pl.Element(win), pl.Element(aw)),
                            lambda bb, r: (bb, rstart(r) * GRID_W, 0))
    ctx_spec = pl.BlockSpec((None, lctx, aw), lambda bb, r: (bb, 0, 0))
    return pl.pallas_call(
        _na_kernel,
        grid=(b, rows),
        in_specs=[pl.BlockSpec((None, GRID_W, aw), lambda bb, r: (bb, r, 0)),
                  win_spec, win_spec, ctx_spec, ctx_spec,
                  pl.BlockSpec((None, NA_HEADS, GRID_W, win), lambda bb, r: (r - rstart(r), 0, 0, 0))],
        out_specs=pl.BlockSpec((None, GRID_W, aw), lambda bb, r: (bb, r, 0)),
        out_shape=jax.ShapeDtypeStruct((b, n, aw), BF16),
        compiler_params=_cparams(("parallel", "arbitrary")),
        name="na_attn",
    )(qa, ka, va, kac, vac, bias_tab)


def _na_bias_table(rpb):
    kr, kc = NA_WIN_ROWS, NA_WIN_COLS
    c_idx = np.arange(GRID_W)
    c_start = np.clip(c_idx - kc // 2, 0, GRID_W - kc)
    kcol = np.arange(GRID_W)
    inside = (kcol[None, :] >= c_start[:, None]) & (kcol[None, :] < c_start[:, None] + kc)
    col_off = np.clip(kcol[None, :] - c_idx[:, None] + (NA_WIN_COLS - 1), 0, 2 * NA_WIN_COLS - 2)
    cls = np.arange(kr)[:, None]
    j = np.arange(kr)[None, :]
    row_off = j - cls + (NA_WIN_ROWS - 1)
    t = rpb[:, row_off][:, :, :, col_off]
    t = jnp.where(jnp.asarray(inside)[None, None, None], t, -1e30)
    t = jnp.transpose(t, (1, 0, 3, 2, 4))
    return t.reshape(kr, rpb.shape[0], GRID_W, kr * GRID_W).astype(F32)


def _flash_kernel(*refs, tq, tk, nk_main, ngroups, diff, lam_init):
    it = iter(refs)
    q_ref = next(it)
    kc_ref = next(it)
    vc_ref = next(it)
    k_ref = v_ref = None
    if nk_main:
        k_ref = next(it)
        v_ref = next(it)
    if diff:
        lq1_ref, lk1_ref, lq2_ref, lk2_ref, gsub_ref = (next(it) for _ in range(5))
    o_ref = next(it)
    scratch = list(it)
    state, pipe = scratch[:3 * ngroups], scratch[3 * ngroups:]

    qt = q_ref[...]
    if diff:
        row = lax.broadcasted_iota(I32, qt.shape, 0)
        zero = jnp.zeros_like(qt)
        qt = jnp.concatenate([jnp.where(row < DIFF_D, qt, zero), jnp.where(row >= DIFF_D, qt, zero)], axis=1)
    gw = qt.shape[1] // ngroups
    for g in range(ngroups):
        m_ref, l_ref, acc_ref = state[3 * g:3 * g + 3]
        m_ref[...] = jnp.full(m_ref.shape, -1e30, F32)
        l_ref[...] = jnp.zeros(l_ref.shape, F32)
        acc_ref[...] = jnp.zeros(acc_ref.shape, F32)

    def block(kb, vts):
        s, pb, alpha = {}, {}, {}
        for t in range(ngroups + 2):
            if t < ngroups:
                s[t] = _dot(kb, qt[:, t * gw:(t + 1) * gw])
            g = t - 1
            if 0 <= g < ngroups:
                m_ref, l_ref, _ = state[3 * g:3 * g + 3]
                m_old = m_ref[...]
                m_new = jnp.maximum(m_old, jnp.max(s[g], axis=0, keepdims=True))
                alpha[g] = jnp.exp2(m_old - m_new)
                p = jnp.exp2(s.pop(g) - m_new)
                l_ref[...] = alpha[g] * l_ref[...] + jnp.sum(p, axis=0, keepdims=True)
                m_ref[...] = m_new
                pb[g] = p.astype(BF16)
            g = t - 2
            if 0 <= g < ngroups:
                acc_ref = state[3 * g + 2]
                pg = pb.pop(g)
                ck = vts[0].shape[1]
                pv = _dot(vts[0], pg[:ck])
                for c in range(1, len(vts)):
                    pv = pv + _dot(vts[c], pg[c * ck:(c + 1) * ck])
                acc_ref[...] = alpha.pop(g) * acc_ref[...] + pv

    block(kc_ref[...], [vc_ref[0]])
    if nk_main and not pipe:
        nch = tk // v_ref.shape[2]

        def body(j, carry):
            off = pl.multiple_of(j * tk, tk)
            block(k_ref[pl.ds(off, tk), :], [v_ref[j * nch + c] for c in range(nch)])
            return carry
        lax.fori_loop(0, nk_main // tk, body, 0)
    elif nk_main:
        ck = v_ref.shape[2]
        nch = tk // ck
        nblocks = nk_main // tk

        def scores(j, g):
            s_buf, cmax_buf = pipe[4 * g], pipe[4 * g + 1]
            off = j * tk
            if not isinstance(off, int):
                off = pl.multiple_of(off, tk)
            s = _dot(k_ref[pl.ds(off, tk), :], qt[:, g * gw:(g + 1) * gw])
            s_buf[...] = s
            cmax_buf[...] = jnp.max(s, axis=0, keepdims=True)

        def softmax(g):
            m_ref, l_ref, _ = state[3 * g:3 * g + 3]
            s_buf, cmax_buf, p_buf, a_buf = pipe[4 * g:4 * g + 4]
            m_old = m_ref[...]
            m_new = jnp.maximum(m_old, cmax_buf[...])
            alpha = jnp.exp2(m_old - m_new)
            p = jnp.exp2(s_buf[...] - m_new)
            l_ref[...] = alpha * l_ref[...] + jnp.sum(p, axis=0, keepdims=True)
            m_ref[...] = m_new
            p_buf[...] = p.astype(BF16)
            a_buf[...] = alpha

        def values(j, g):
            acc_ref, p_buf, a_buf = state[3 * g + 2], pipe[4 * g + 2], pipe[4 * g + 3]
            pv = _dot(v_ref[j * nch], p_buf[0:ck, :])
            for c in range(1, nch):
                pv = pv + _dot(v_ref[j * nch + c], p_buf[c * ck:(c + 1) * ck, :])
            acc_ref[...] = a_buf[...] * acc_ref[...] + pv

        def key_block(j, last):
            for g in range(ngroups):
                if not (last and g + 2 >= ngroups):
                    scores(j + (g + 2) // ngroups, (g + 2) % ngroups)
                if not (last and g + 1 >= ngroups):
                    softmax((g + 1) % ngroups)
                values(j, g)

        scores(0, 0)
        scores(0, 1)
        softmax(0)

        def body(j, carry):
            key_block(j, False)
            return carry
        lax.fori_loop(0, nblocks - 1, body, 0)
        key_block(nblocks - 1, True)

    o = jnp.concatenate([state[3 * g + 2][...] / state[3 * g + 1][...] for g in range(ngroups)], axis=1)
    if diff:
        lam = (jnp.exp(jnp.sum(lq1_ref[...] * lk1_ref[...], axis=-1, keepdims=True))
               - jnp.exp(jnp.sum(lq2_ref[...] * lk2_ref[...], axis=-1, keepdims=True)) + lam_init)
        o = o[:, :tq] - lam * o[:, tq:]
        o = o * lax.rsqrt(jnp.mean(o * o, axis=0, keepdims=True) + EPS) * gsub_ref[...] * (1.0 - lam_init)
    o_ref[...] = o.T.astype(BF16)


def _pick_tile(n, cands):
    for c in cands:
        if n % c == 0:
            return c
    return n


def _flash_call(qt, kc, vtc, k, vt, *, tq, diff=False, lam_params=None, lam_init=0.0, name="flash"):
    b, nh, _, n = qt.shape
    hw = nh * HEAD_PAD
    lctx = kc.shape[1]
    nk_main = 0 if k is None else k.shape[1]
    tq = min(tq, n)
    tk = 0
    if nk_main:
        chunk = vt.shape[-1]
        tk = chunk * _pick_tile(nk_main // chunk, (2, 1))
    m_rows = 2 * tq if diff else tq
    gw = min(QUERY_GROUP, m_rows)
    qspec = pl.BlockSpec((None, None, HEAD_PAD, tq), lambda bb, hh, i: (bb, hh, 0, i))
    ospec = pl.BlockSpec((None, tq, HEAD_PAD), lambda bb, hh, i: (bb, i, hh))
    in_specs = [qspec,
                pl.BlockSpec((None, lctx, HEAD_PAD), lambda bb, hh, i: (bb, 0, hh)),
                pl.BlockSpec((None, None, 1, HEAD_PAD, lctx), lambda bb, hh, i: (bb, hh, 0, 0, 0))]
    args = [qt, kc, vtc]
    if nk_main:
        in_specs += [pl.BlockSpec((None, nk_main, HEAD_PAD), lambda bb, hh, i: (bb, 0, hh)),
                     pl.BlockSpec((None, None) + vt.shape[2:], lambda bb, hh, i: (bb, hh, 0, 0, 0))]
        args += [k, vt]
    if diff:
        small = lambda a: pl.BlockSpec(a.shape, lambda bb, hh, i: (0,) * a.ndim)
        in_specs += [small(a) for a in lam_params]
        args += list(lam_params)
    ngroups = m_rows // gw
    scratch = [pltpu.VMEM(shape, F32) for _ in range(ngroups) for shape in ((1, gw), (1, gw), (HEAD_PAD, gw))]
    if nk_main and ngroups >= 3:
        scratch += [pltpu.VMEM(shape, dt) for _ in range(ngroups)
                    for shape, dt in (((tk, gw), F32), ((1, gw), F32), ((tk, gw), BF16), ((1, gw), F32))]
    kern = functools.partial(_flash_kernel, tq=tq, tk=tk, nk_main=nk_main, ngroups=ngroups, diff=diff,
                             lam_init=lam_init)
    return pl.pallas_call(
        kern,
        grid=(b, nh, n // tq),
        in_specs=in_specs,
        out_specs=ospec,
        out_shape=jax.ShapeDtypeStruct((b, n, hw), BF16),
        scratch_shapes=scratch,
        compiler_params=_cparams(("parallel", "parallel", "arbitrary")),
        name=name,
    )(*args)


def _merge_kernel(x_ref, sh1_ref, sc1_ref, g1_ref, sh2_ref, sc2_ref, ya_ref, yb_ref, yc_ref,
                  wg_ref, bg_ref, wa_ref, wb_ref, wc_ref, wo_ref, lng_ref, lnb_ref, wr_ref,
                  x1_ref, h2_ref, aff_ref, *, alpha):
    x = x_ref[...]
    d = x.shape[-1]
    hb = (_ln_core(x) * (1.0 + sc1_ref[...]) + sh1_ref[...]).astype(BF16)

    def gate(i):
        return jax.nn.sigmoid(_dot(hb, wg_ref[:, i * d:(i + 1) * d]) + bg_ref[:, i * d:(i + 1) * d])

    m = gate(0) * _dot(ya_ref[...], wa_ref[...])
    m = m + gate(1) * _dot(yb_ref[...], wb_ref[...])
    m = m + gate(2) * _dot(yc_ref[...], wc_ref[...])
    y = _dot(m.astype(BF16), wo_ref[...])
    x1 = _ln_core(alpha * x + g1_ref[...] * y) * lng_ref[...] + lnb_ref[...]
    x1_ref[...] = x1
    h2 = _ln_core(x1) * (1.0 + sc2_ref[...]) + sh2_ref[...]
    h2_ref[...] = h2.astype(BF16)
    logits = jnp.dot(h2, wr_ref[...], precision=lax.Precision.HIGHEST, preferred_element_type=F32)
    lane = lax.broadcasted_iota(I32, logits.shape, 1)
    logits = jnp.where(lane < N_EXPERTS, logits, -1e30)
    e = jnp.exp(logits - jnp.max(logits, axis=-1, keepdims=True))
    aff = e / jnp.sum(e, axis=-1, keepdims=True)
    aff_ref[...] = aff[:, :N_EXPERTS]


def _merge_call(x, mods, ya, yb, yc, wp, alpha, tm):
    b, n, d = x.shape
    tm = min(tm, n)
    row = lambda w: pl.BlockSpec((None, tm, w), lambda bb, i: (bb, i, 0))
    vec = pl.BlockSpec((None, 1, d), lambda bb, i: (bb, 0, 0))
    full = lambda a: pl.BlockSpec(a.shape, lambda bb, i: (0,) * a.ndim)
    ws = [wp["w_gate"], wp["b_gate"], wp["w_br_a"], wp["w_br_b"], wp["w_br_c"], wp["w_out"],
          wp["ln1_g"], wp["ln1_b"], wp["w_router"]]
    return pl.pallas_call(
        functools.partial(_merge_kernel, alpha=alpha),
        grid=(b, n // tm),
        in_specs=[row(d)] + [vec] * 5 + [row(ya.shape[-1]), row(yb.shape[-1]), row(yc.shape[-1])]
        + [full(a) for a in ws],
        out_specs=[row(d), row(d), row(N_EXPERTS)],
        out_shape=[jax.ShapeDtypeStruct((b, n, d), F32), jax.ShapeDtypeStruct((b, n, d), BF16),
                   jax.ShapeDtypeStruct((b, n, N_EXPERTS), F32)],
        compiler_params=_cparams(("parallel", "parallel")),
        name="merge",
    )(x, *mods, ya, yb, yc, *ws)


def _topk_kernel(a_ref, posm_ref, pos_ref, *, cap):
    nck = a_ref.shape[0]

    def count(pred):
        bits = lax.bitcast_convert_type(a_ref[...], I32)
        c = jnp.sum(pred(bits).astype(F32), axis=0)
        return jnp.sum(c, axis=-1, keepdims=True)

    def bit_step(i, t):
        cand = t | jnp.left_shift(jnp.int32(1), 30 - i)
        return jnp.where(count(lambda bits: bits >= cand[None]) >= cap, cand, t)

    t = lax.fori_loop(0, 31, bit_step, jnp.zeros((N_EXPERTS, 1), I32))
    need = cap - count(lambda bits: bits > t[None])

    r = lax.broadcasted_iota(I32, (LANE, LANE), 0)
    c = lax.broadcasted_iota(I32, (LANE, LANE), 1)
    u_incl = (r <= c).astype(BF16)
    u_excl = (r < c).astype(BF16)

    def chunk_step(ci, carry):
        run_eq, run_sel = carry
        bc = lax.bitcast_convert_type(a_ref[ci], I32)
        gt = bc > t
        eq = bc == t
        eq_before = _dot(eq.astype(BF16), u_excl) + run_eq
        sel = gt | (eq & (eq_before < need))
        sel_incl = (_dot(sel.astype(BF16), u_incl) + run_sel).astype(I32)
        pos_ref[ci] = sel_incl
        posm_ref[ci] = jnp.where(sel, sel_incl - 1, -1)
        run_eq = run_eq + jnp.sum(eq.astype(F32), axis=-1, keepdims=True)
        run_sel = run_sel + jnp.sum(sel.astype(F32), axis=-1, keepdims=True)
        return run_eq, run_sel

    z = jnp.zeros((N_EXPERTS, 1), F32)
    lax.fori_loop(0, nck, chunk_step, (z, z))


def _topk_call(aff_ck, cap):
    b, nck, e, _ = aff_ck.shape
    spec = pl.BlockSpec((None, nck, e, LANE), lambda bb: (bb, 0, 0, 0))
    return pl.pallas_call(
        functools.partial(_topk_kernel, cap=cap),
        grid=(b,),
        in_specs=[spec],
        out_specs=[spec, spec],
        out_shape=[jax.ShapeDtypeStruct(aff_ck.shape, I32)] * 2,
        compiler_params=_cparams(("parallel",)),
        name="topk",
    )(aff_ck)


def _gather_kernel(chunk_ref, tile_ref, valid_ref, first_ref, posm_ref, h_ref, o_ref, acc_ref, *, cj, steps):
    bb, e, s = pl.program_id(0), pl.program_id(1), pl.program_id(2)
    idx = (bb * pl.num_programs(1) + e) * steps + s

    @pl.when(first_ref[idx] == 1)
    def _():
        acc_ref[...] = jnp.zeros_like(acc_ref)

    @pl.when(valid_ref[idx] == 1)
    def _():
        tt = h_ref.shape[0]
        slot = chunk_ref[idx] * cj + lax.broadcasted_iota(I32, (cj, tt), 0)
        onehot = (posm_ref[...] == slot).astype(BF16)
        acc_ref[...] += _dot(onehot, h_ref[...])

    o_ref[...] = acc_ref[...].astype(BF16)


def _gather_call(wl, posm_rows, h2, cap, cj, tt):
    b, n, d = h2.shape
    e = N_EXPERTS
    steps = wl["g_chunk"].shape[0] // (b * e)
    grid_spec = pltpu.PrefetchScalarGridSpec(
        num_scalar_prefetch=4,
        grid=(b, e, steps),
        in_specs=[
            pl.BlockSpec((None, None, None, 1, tt),
                         lambda bb, ee, s, ch, tl, va, fi: (bb, ee, tl[(bb * e + ee) * steps + s], 0, 0)),
            pl.BlockSpec((None, tt, d), lambda bb, ee, s, ch, tl, va, fi: (bb, tl[(bb * e + ee) * steps + s], 0)),
        ],
        out_specs=pl.BlockSpec((None, None, cj, d),
                               lambda bb, ee, s, ch, tl, va, fi: (bb, ee, ch[(bb * e + ee) * steps + s], 0)),
        scratch_shapes=[pltpu.VMEM((cj, d), F32)],
    )
    return pl.pallas_call(
        functools.partial(_gather_kernel, cj=cj, steps=steps),
        grid_spec=grid_spec,
        out_shape=jax.ShapeDtypeStruct((b, e, cap, d), BF16),
        compiler_params=_cparams(("arbitrary", "arbitrary", "arbitrary")),
        name="moe_gather",
    )(wl["g_chunk"], wl["g_tile"], wl["g_valid"], wl["g_first"], posm_rows, h2)


def _ffn_kernel(x_ref, wg_ref, wu_ref, wd_ref, o_ref, acc_ref):
    f = pl.program_id(2)

    @pl.when(f == 0)
    def _():
        acc_ref[...] = jnp.zeros_like(acc_ref)

    x = x_ref[...]
    a = _dot(x, wg_ref[...].astype(BF16))
    u = _dot(x, wu_ref[...].astype(BF16))
    hmid = (a * jax.nn.sigmoid(a) * u).astype(BF16)
    acc_ref[...] += _dot(hmid, wd_ref[...].astype(BF16))

    @pl.when(f == pl.num_programs(2) - 1)
    def _():
        o_ref[...] = acc_ref[...].astype(BF16)


def _ffn_call(xsel, w_g, w_u, w_d):
    b, e, cap, d = xsel.shape
    ff = w_g.shape[-1]
    tf = _pick_tile(ff, (256, 128))
    xspec = pl.BlockSpec((None, None, cap, d), lambda ee, bb, f: (bb, ee, 0, 0))
    return pl.pallas_call(
        _ffn_kernel,
        grid=(e, b, ff // tf),
        in_specs=[xspec,
                  pl.BlockSpec((None, d, tf), lambda ee, bb, f: (ee, 0, f)),
                  pl.BlockSpec((None, d, tf), lambda ee, bb, f: (ee, 0, f)),
                  pl.BlockSpec((None, tf, d), lambda ee, bb, f: (ee, f, 0))],
        out_specs=xspec,
        out_shape=jax.ShapeDtypeStruct(xsel.shape, BF16),
        scratch_shapes=[pltpu.VMEM((cap, d), F32)],
        compiler_params=_cparams(("parallel", "parallel", "arbitrary")),
        name="moe_ffn",
    )(xsel, w_g, w_u, w_d)


def _combine_kernel(tile_ref, exp_ref, chunk_ref, valid_ref, first_ref, last_ref,
                    posm_ref, aff_ref, y_ref, x_ref, g2_ref, lng_ref, lnb_ref, o_ref, acc_ref,
                    *, cj, steps, alpha):
    idx = pl.program_id(0) * steps + pl.program_id(1)

    @pl.when(first_ref[idx] == 1)
    def _():
        acc_ref[...] = jnp.zeros_like(acc_ref)

    @pl.when(valid_ref[idx] == 1)
    def _():
        tt = acc_ref.shape[0]
        lane = lax.broadcasted_iota(I32, (tt, N_EXPERTS), 1)
        pick = lane == exp_ref[idx]
        slot = jnp.sum(jnp.where(pick, posm_ref[...].astype(F32), 0.0), axis=-1, keepdims=True)
        gate = jnp.sum(jnp.where(pick, aff_ref[...], 0.0), axis=-1, keepdims=True)
        cols = chunk_ref[idx] * cj + lax.broadcasted_iota(I32, (tt, cj), 1)
        onehot = (slot == cols.astype(F32)).astype(BF16)
        acc_ref[...] += gate * _dot(onehot, y_ref[...])

    @pl.when(last_ref[idx] == 1)
    def _():
        o_ref[...] = _ln_core(alpha * x_ref[...] + g2_ref[...] * acc_ref[...]) * lng_ref[...] + lnb_ref[...]


def _combine_call(wl, posm_tm, aff_tm, y, x1, g2, ln_g, ln_b, alpha, cj, tt):
    b, n, d = x1.shape
    e = N_EXPERTS
    steps = wl["c_tile"].shape[0] // b
    tok = lambda w: pl.BlockSpec((None, tt, w), lambda bb, s, tl, ex, ch, va, fi, la: (bb, tl[bb * steps + s], 0))
    small = lambda a: pl.BlockSpec(a.shape, lambda bb, s, tl, ex, ch, va, fi, la: (0,) * a.ndim)
    grid_spec = pltpu.PrefetchScalarGridSpec(
        num_scalar_prefetch=6,
        grid=(b, steps),
        in_specs=[tok(e), tok(e),
                  pl.BlockSpec((None, None, cj, d),
                               lambda bb, s, tl, ex, ch, va, fi, la: (bb, ex[bb * steps + s], ch[bb * steps + s], 0)),
                  tok(d),
                  pl.BlockSpec((None, 1, d), lambda bb, s, tl, ex, ch, va, fi, la: (bb, 0, 0)),
                  small(ln_g), small(ln_b)],
        out_specs=tok(d),
        scratch_shapes=[pltpu.VMEM((tt, d), F32)],
    )
    return pl.pallas_call(
        functools.partial(_combine_kernel, cj=cj, steps=steps, alpha=alpha),
        grid_spec=grid_spec,
        out_shape=jax.ShapeDtypeStruct((b, n, d), F32),
        compiler_params=_cparams(("arbitrary", "arbitrary")),
        name="moe_combine",
    )(wl["c_tile"], wl["c_exp"], wl["c_chunk"], wl["c_valid"], wl["c_first"], wl["c_last"],
      posm_tm, aff_tm, y, x1, g2, ln_g, ln_b)


def _work_lists(pos_rows, cap, cj, tt):
    b, e, n = pos_rows.shape
    nt, nc = n // tt, cap // cj
    pend = pos_rows[:, :, tt - 1::tt]
    pstart = jnp.concatenate([jnp.zeros((b, e, 1), I32), pend[:, :, :-1]], axis=-1)
    nonempty = pend > pstart
    clo = jnp.minimum(pstart // cj, nc - 1)
    chi = jnp.maximum(pend - 1, 0) // cj
    npairs = jnp.where(nonempty, chi - clo + 1, 0)

    sg = nt + nc
    cum = jnp.cumsum(npairs, axis=-1)
    total = cum[..., -1:]
    s = jnp.arange(sg, dtype=I32)[None, None, :]
    s_eff = jnp.minimum(s, total - 1)
    tile = jnp.sum((cum[..., None, :] <= s_eff[..., :, None]).astype(I32), axis=-1)
    off = jnp.take_along_axis(cum - npairs, tile, axis=-1)
    chunk = jnp.take_along_axis(clo, tile, axis=-1) + (s_eff - off)
    valid = (s < total).astype(I32)
    prev = jnp.concatenate([jnp.full((b, e, 1), -1, I32), chunk[..., :-1]], axis=-1)
    first = (chunk != prev).astype(I32)

    sc = e * (nt + nc)
    nent = jnp.maximum(jnp.transpose(npairs, (0, 2, 1)).reshape(b, nt * e), 1)
    npf = jnp.transpose(npairs, (0, 2, 1)).reshape(b, nt * e)
    clof = jnp.transpose(clo, (0, 2, 1)).reshape(b, nt * e)
    ccum = jnp.cumsum(nent, axis=-1)
    ctotal = ccum[:, -1:]
    s2 = jnp.arange(sc, dtype=I32)[None, :]
    s2_eff = jnp.minimum(s2, ctotal - 1)
    ent = jnp.sum((ccum[:, None, :] <= s2_eff[:, :, None]).astype(I32), axis=-1)
    coff = jnp.take_along_axis(ccum - nent, ent, axis=-1)
    c_tile = ent // e
    c_exp = ent % e
    c_chunk = jnp.minimum(jnp.take_along_axis(clof, ent, axis=-1) + (s2_eff - coff), nc - 1)
    live = s2 < ctotal
    c_valid = (live & (jnp.take_along_axis(npf, ent, axis=-1) > 0)).astype(I32)
    tprev = jnp.concatenate([jnp.full((b, 1), -1, I32), c_tile[:, :-1]], axis=-1)
    tnext = jnp.concatenate([c_tile[:, 1:], jnp.full((b, 1), -1, I32)], axis=-1)
    c_first = (live & (c_tile != tprev)).astype(I32)
    c_last = (live & ((c_tile != tnext) | (s2 == ctotal - 1))).astype(I32)
    flat = lambda a: a.reshape(-1).astype(I32)
    return dict(g_chunk=flat(chunk), g_tile=flat(tile), g_valid=flat(valid), g_first=flat(first),
                c_tile=flat(c_tile), c_exp=flat(c_exp), c_chunk=flat(c_chunk), c_valid=flat(c_valid),
                c_first=flat(c_first), c_last=flat(c_last))


def _moe(h2, aff_tm, x1, g2, ln_g, ln_b, w_g, w_u, w_d, alpha):
    b, n, d = h2.shape
    e = N_EXPERTS
    cap = max(1, EC_CAPACITY_FACTOR * n // e)
    cj = min(SLOT_CHUNK, cap)
    tt = min(TOKEN_TILE, n)
    nck = n // LANE
    aff_ck = jnp.transpose(aff_tm.reshape(b, nck, LANE, e), (0, 1, 3, 2))
    posm_ck, pos_ck = _topk_call(aff_ck, cap)
    posm_rows = jnp.transpose(posm_ck, (0, 2, 1, 3)).reshape(b, e, n)
    pos_rows = jnp.transpose(pos_ck, (0, 2, 1, 3)).reshape(b, e, n)
    posm_tm = jnp.transpose(posm_rows, (0, 2, 1))
    wl = _work_lists(pos_rows, cap, cj, tt)
    xsel = _gather_call(wl, posm_rows.reshape(b, e, n // tt, 1, tt), h2, cap, cj, tt)
    y = _ffn_call(xsel, w_g, w_u, w_d)
    return _combine_call(wl, posm_tm, aff_tm, y, x1, g2, ln_g, ln_b, alpha, cj, tt)


def _swap_perm(rot_dim):
    q = rot_dim // 4
    return np.concatenate([np.arange(q, 2 * q), np.arange(0, q), np.arange(3 * q, 4 * q), np.arange(2 * q, 3 * q)])


def _rope_tables(n, rot_dim):
    t = jnp.arange(n, dtype=I32)
    row = (t // GRID_W).astype(F32)
    col = (t % GRID_W).astype(F32)
    m = rot_dim // 2
    inv = ROPE_BASE ** (-jnp.arange(0, m, 2, dtype=F32) / m)
    ar, ac = row[:, None] * inv, col[:, None] * inv
    cos = jnp.concatenate([jnp.cos(ar), jnp.cos(ar), jnp.cos(ac), jnp.cos(ac)], axis=-1)
    sin = jnp.concatenate([-jnp.sin(ar), jnp.sin(ar), -jnp.sin(ac), jnp.sin(ac)], axis=-1)
    return cos, sin


def _head_tables(n, rope):
    if rope:
        cb, sb = _rope_tables(n, MLA_ROPE)
        cdd, sdd = _rope_tables(n, DIFF_D)
    else:
        cb, sb = jnp.ones((n, MLA_ROPE), F32), jnp.zeros((n, MLA_ROPE), F32)
        cdd, sdd = jnp.ones((n, DIFF_D), F32), jnp.zeros((n, DIFF_D), F32)
    pad = HEAD_PAD - MLA_NOPE - MLA_ROPE
    cm = jnp.concatenate([jnp.ones((n, MLA_NOPE), F32), cb, jnp.zeros((n, pad), F32)], axis=-1)
    sm = jnp.concatenate([jnp.zeros((n, MLA_NOPE), F32), sb, jnp.zeros((n, pad), F32)], axis=-1)
    return cm, sm, jnp.concatenate([cdd, cdd], axis=-1), jnp.concatenate([sdd, sdd], axis=-1)


def _prep_layer(w_in, g_q, g_kv, w_uq, w_ukv, w_br_b, w_router):
    d = w_in.shape[0]
    offs = np.cumsum([0, A_W, A_W, A_W, MLA_Q_RANK, MLA_KV_RANK, MLA_ROPE, C_QK, C_QK, C_W])
    seg = lambda i: w_in[:, offs[i]:offs[i + 1]]
    z = lambda w: jnp.zeros((d, w), w_in.dtype)
    sw32 = _swap_perm(MLA_ROPE)
    sw64 = _swap_perm(DIFF_D)
    chunk_sw = np.concatenate([c * DIFF_D + sw64 for c in range(C_QK // DIFF_D)])
    kr = seg(5)
    pad = HEAD_PAD - MLA_NOPE - MLA_ROPE
    w_ext = jnp.concatenate([
        seg(0), seg(1), seg(2), seg(3), seg(4),
        z(MLA_NOPE), kr, z(pad), z(MLA_NOPE), kr[:, sw32], z(pad),
        seg(6), seg(6)[:, chunk_sw], seg(7), seg(7)[:, chunk_sw], seg(8)], axis=1).astype(BF16)
    assert w_ext.shape[1] == W_EXT

    qd = MLA_NOPE + MLA_ROPE
    wq = w_uq.reshape(MLA_Q_RANK, MLA_HEADS, qd)
    zq = lambda w: jnp.zeros((MLA_Q_RANK, MLA_HEADS, w), w_uq.dtype)
    wuq1 = jnp.concatenate([wq, zq(pad)], axis=-1).reshape(MLA_Q_RANK, MLA_HEADS * HEAD_PAD).astype(BF16)
    wuq2 = jnp.concatenate([zq(MLA_NOPE), wq[:, :, MLA_NOPE:][:, :, sw32], zq(pad)],
                           axis=-1).reshape(MLA_Q_RANK, MLA_HEADS * HEAD_PAD).astype(BF16)
    wkv = w_ukv.reshape(MLA_KV_RANK, MLA_HEADS, MLA_NOPE + MLA_V)
    zk = jnp.zeros((MLA_KV_RANK, MLA_HEADS, HEAD_PAD - MLA_NOPE), w_ukv.dtype)
    zv = jnp.zeros((MLA_KV_RANK, MLA_HEADS, HEAD_PAD - MLA_V), w_ukv.dtype)
    wkvk = jnp.concatenate([wkv[:, :, :MLA_NOPE], zk], axis=-1).reshape(MLA_KV_RANK, -1).astype(BF16)
    wkvv = jnp.concatenate([wkv[:, :, MLA_NOPE:], zv], axis=-1).reshape(MLA_KV_RANK, -1).astype(BF16)
    wb = w_br_b.reshape(MLA_HEADS, MLA_V, d)
    wb = jnp.concatenate([wb, jnp.zeros((MLA_HEADS, HEAD_PAD - MLA_V, d), wb.dtype)], axis=1)
    wb = wb.reshape(MLA_HEADS * HEAD_PAD, d).astype(BF16)
    wr = jnp.concatenate([w_router, jnp.zeros((d, LANE - N_EXPERTS), w_router.dtype)], axis=1)
    return dict(w_in=w_ext, g_q=g_q.reshape(1, -1), g_kv=g_kv.reshape(1, -1), wuq1=wuq1, wuq2=wuq2,
                wkvk=wkvk, wkvv=wkvv, w_br_b=wb, w_router=wr)


def _pad_heads(t, nh):
    b, n, w = t.shape
    hd = w // nh
    t = t.reshape(b, n, nh, hd)
    t = jnp.concatenate([t, jnp.zeros((b, n, nh, HEAD_PAD - hd), t.dtype)], axis=-1)
    return t.reshape(b, n, nh * HEAD_PAD)


def _unpad_heads(t, nh, hd):
    b, n, _ = t.shape
    return t.reshape(b, n, nh, HEAD_PAD)[..., :hd].reshape(b, n, nh * hd)


def kernel(x, c, ctx, c_ctx, w_ada, b_ada, w_in, na_rpb, mla_g_q, mla_g_kv, mla_w_uq, mla_w_ukv,
           diff_lq1, diff_lk1, diff_lq2, diff_lk2, diff_g_sub, w_br_a, w_br_b, w_br_c,
           w_gate, b_gate, w_out, ln1_g, ln1_b, w_router, w_exp_gate, w_exp_up, w_exp_down,
           ln2_g, ln2_b):
    depth = w_ada.shape[0]
    b, n, d = x.shape
    lctx = ctx.shape[1]
    alpha = (2 * depth) ** 0.25

    cvec = jnp.concatenate([c, jnp.broadcast_to(c_ctx[None], (8 - b, d))], axis=0)
    mod_all = _ada_call(cvec, w_ada, b_ada)
    tabs_x = _head_tables(n, True)
    tabs_c = _head_tables(lctx, False)

    for l in range(depth):
        last = l == depth - 1
        lam_init = 0.8 - 0.6 * math.exp(-0.3 * l)
        mod = mod_all[l, :b].reshape(b, 1, 6 * d)
        mod_c = jnp.broadcast_to(mod_all[l, b:b + 1].reshape(1, 1, 6 * d), (b, 1, 6 * d))
        mods = [mod[..., i * d:(i + 1) * d] for i in range(6)]
        mods_c = [mod_c[..., i * d:(i + 1) * d] for i in range(6)]
        wp = _prep_layer(w_in[l], mla_g_q[l], mla_g_kv[l], mla_w_uq[l], mla_w_ukv[l], w_br_b[l], w_router[l])
        wp.update(w_gate=w_gate[l].astype(BF16), b_gate=b_gate[l].reshape(1, -1),
                  w_br_a=w_br_a[l].astype(BF16), w_br_c=w_br_c[l].astype(BF16), w_out=w_out[l].astype(BF16),
                  ln1_g=ln1_g[l].reshape(1, -1), ln1_b=ln1_b[l].reshape(1, -1))
        lam_params = [diff_lq1[l].reshape(1, -1), diff_lk1[l].reshape(1, -1), diff_lq2[l].reshape(1, -1),
                      diff_lk2[l].reshape(1, -1), diff_g_sub[l].reshape(-1, 1)]
        ln2g, ln2b = ln2_g[l].reshape(1, -1), ln2_b[l].reshape(1, -1)

        zx = _inproj_call(x, mods[0], mods[1], wp, tabs_x, min(INPROJ_TILE, n))
        zc = _inproj_call(ctx, mods_c[0], mods_c[1], wp, tabs_c, lctx)
        qa, ka, va, qm, km, vm, dq, dk, dv = zx
        qa_c, ka_c, va_c, qm_c, km_c, vm_c, dq_c, dk_c, dv_c = zc

        ya = _na_call(qa, ka, va, ka_c, va_c, _na_bias_table(na_rpb[l]))
        yb = _flash_call(qm, km_c, vm_c, km, vm, tq=4 * QUERY_GROUP, name="mla_attn")
        yc = _flash_call(dq, dk_c, dv_c, dk, dv, tq=2 * QUERY_GROUP, diff=True, lam_params=lam_params,
                         lam_init=lam_init, name="diff_attn")
        x1, h2, aff = _merge_call(x, mods[:5], ya, yb, yc, wp, alpha, 512)
        x = _moe(h2, aff, x1, mods[5], ln2g, ln2b, w_exp_gate[l], w_exp_up[l], w_exp_down[l], alpha)

        if not last:
            heads_t = lambda t: jnp.transpose(_pad_heads(t, NA_HEADS).reshape(b, lctx, NA_HEADS, HEAD_PAD),
                                              (0, 2, 3, 1))
            qa_c2 = (qa_c.astype(F32) * LOG2E).astype(BF16)
            ya_c = _unpad_heads(
                _flash_call(heads_t(qa_c2), _pad_heads(ka_c, NA_HEADS), heads_t(va_c)[:, :, None],
                            None, None, tq=lctx, name="ctx_na_attn"), NA_HEADS, HEAD_DIM)
            yb_c = _flash_call(qm_c, km_c, vm_c, None, None, tq=lctx, name="ctx_mla_attn")
            yc_c = _flash_call(dq_c, dk_c, dv_c, None, None, tq=lctx, diff=True, lam_params=lam_params,
                               lam_init=lam_init, name="ctx_diff_attn")
            c1, hc2, aff_c = _merge_call(ctx, mods_c[:5], ya_c, yb_c, yc_c, wp, alpha, lctx)
            ctx = _moe(hc2, aff_c, c1, mods_c[5], ln2g, ln2b, w_exp_gate[l], w_exp_up[l], w_exp_down[l], alpha)
    return x
```
